```python
import math
import jax
import jax.numpy as jnp
from jax import lax
import numpy as np

D_MODEL = 4096
BATCH = 1
SEQ = 8192
DEPTH = 1

HEAD_DIM = 128
A_Q_HEADS = 16
A_KV_HEADS = 4
A_GROUP = A_Q_HEADS // A_KV_HEADS
A_WINDOW = 128
B_PATTERNS = ((128, 1), (512, 4), (2048, 16))
B_HEADS_PER_GROUP = 4
B_HEADS = B_HEADS_PER_GROUP * len(B_PATTERNS)
ROPE_THETA = 500000.0
ROPE_DIMS = HEAD_DIM // 4
BLOCK = 128
EPS = 1e-6
PEER_HEADS = 8
PEER_N_KEYS = 128
PEER_N_EXPERTS = PEER_N_KEYS * PEER_N_KEYS
PEER_QUERY_DIM = 256
PEER_HALF = PEER_QUERY_DIM // 2
PEER_TOPK = 16
PEER_TOKEN_BLOCK = 64

A_Q_W = A_Q_HEADS * HEAD_DIM
A_KV_W = A_KV_HEADS * HEAD_DIM
B_W = B_HEADS * HEAD_DIM
B_OUT_W = B_HEADS_PER_GROUP * HEAD_DIM
IN_W = A_Q_W + 2 * A_KV_W + 3 * B_W

kernel_name = "hybrid_swa_dilated_peer_block"


def rmsnorm(x, gain):
    xf = x.astype(jnp.float32)
    inv = lax.rsqrt(jnp.mean(xf * xf, axis=-1, keepdims=True) + EPS)
    return (xf * inv * gain.astype(jnp.float32)).astype(x.dtype)


def partial_rope(x, positions):
    half = ROPE_DIMS // 2
    inv_freq = jnp.exp(-math.log(ROPE_THETA) * jnp.arange(half, dtype=jnp.float32) / half)
    ang = positions.astype(jnp.float32)[..., None] * inv_freq
    cos = jnp.cos(ang)[:, :, None, :]
    sin = jnp.sin(ang)[:, :, None, :]
    xr = x[..., :ROPE_DIMS].astype(jnp.float32)
    x1, x2 = xr[..., :half], xr[..., half:]
    rot = jnp.concatenate([x1 * cos - x2 * sin, x2 * cos + x1 * sin], axis=-1).astype(x.dtype)
    return jnp.concatenate([rot, x[..., ROPE_DIMS:]], axis=-1)


def banded_window_attention(q, k, v, max_dist, sinks=None):
    B, L, Hkv, G, Dh = q.shape
    nb = -(-L // BLOCK)
    pad = nb * BLOCK - L
    if pad:
        q = jnp.pad(q, ((0, 0), (0, pad), (0, 0), (0, 0), (0, 0)))
        k = jnp.pad(k, ((0, 0), (0, pad), (0, 0), (0, 0)))
        v = jnp.pad(v, ((0, 0), (0, pad), (0, 0), (0, 0)))
    qb = q.reshape(B, nb, BLOCK, Hkv, G, Dh)

    def band(t):
        tb = t.reshape(B, nb, BLOCK, Hkv, Dh)
        prev = jnp.pad(tb, ((0, 0), (1, 0), (0, 0), (0, 0), (0, 0)))[:, :nb]
        return jnp.concatenate([prev, tb], axis=2)

    kb, vb = band(k), band(v)
    s = jnp.einsum('bnqhgd,bnkhd->bnhgqk', qb, kb,
                   preferred_element_type=jnp.float32) * (Dh ** -0.5)
    qi = jnp.arange(BLOCK)[:, None]
    kj = jnp.arange(2 * BLOCK)[None, :]
    dist = BLOCK + qi - kj
    in_band = (dist >= 0) & (dist <= max_dist)
    has_prev = (jnp.arange(nb) > 0)[:, None, None] | (kj >= BLOCK)[None]
    mask = in_band[None] & has_prev
    s = jnp.where(mask[None, :, None, None], s, -jnp.inf)
    s_max = jnp.max(s, axis=-1, keepdims=True)
    if sinks is not None:
        sink = sinks.astype(jnp.float32)[None, None, :, :, None, None]
        m = jnp.maximum(s_max, sink)
        p = jnp.exp(s - m)
        denom = jnp.sum(p, axis=-1, keepdims=True) + jnp.exp(sink - m)
    else:
        m = s_max
        p = jnp.exp(s - m)
        denom = jnp.sum(p, axis=-1, keepdims=True)
    o = jnp.einsum('bnhgqk,bnkhd->bnqhgd', p, vb.astype(jnp.float32)) / jnp.moveaxis(denom, 4, 2)
    lse = jnp.moveaxis((m + jnp.log(denom))[..., 0], 4, 2)
    o = o.reshape(B, nb * BLOCK, Hkv, G, Dh)[:, :L].astype(q.dtype)
    lse = lse.reshape(B, nb * BLOCK, Hkv, G)[:, :L]
    return o, lse


def dilated_window_attention(q, k, v, window, dilation):
    B, S, H, Dh = q.shape
    L = S // dilation

    def to_sub(t):
        return t.reshape(B, L, dilation, H, Dh).transpose(0, 2, 1, 3, 4).reshape(B * dilation, L, H, Dh)

    out, lse = banded_window_attention(to_sub(q)[:, :, :, None], to_sub(k), to_sub(v), window // dilation)
    out = out.reshape(B, dilation, L, H, Dh).transpose(0, 2, 1, 3, 4).reshape(B, S, H, Dh)
    lse = lse.reshape(B, dilation, L, H).transpose(0, 2, 1, 3).reshape(B, S, H)
    return out, lse


def token_mixing(h, positions, w_in, q_norm_a, k_norm_a, sinks_a, q_norm_b, k_norm_b,
                 w_branch_a, w_branch_b, w_gate, b_gate, w_out):
    B, S, _ = h.shape
    proj = h @ w_in
    offs = [A_Q_W, A_Q_W + A_KV_W, A_Q_W + 2 * A_KV_W,
            A_Q_W + 2 * A_KV_W + B_W, A_Q_W + 2 * A_KV_W + 2 * B_W]
    qa, ka, va, qb, kb, vb = jnp.split(proj, offs, axis=-1)

    def heads(t, n):
        return t.reshape(B, S, n, HEAD_DIM)

    qa = partial_rope(rmsnorm(heads(qa, A_Q_HEADS), q_norm_a), positions)
    qa = qa.reshape(B, S, A_KV_HEADS, A_GROUP, HEAD_DIM)
    ka = partial_rope(rmsnorm(heads(ka, A_KV_HEADS), k_norm_a), positions)
    va = heads(va, A_KV_HEADS)
    ya, _ = banded_window_attention(qa, ka, va, A_WINDOW - 1, sinks_a.reshape(A_KV_HEADS, A_GROUP))
    ya = ya.reshape(B, S, A_Q_W)

    qb = partial_rope(rmsnorm(heads(qb, B_HEADS), q_norm_b), positions)
    kb = partial_rope(rmsnorm(heads(kb, B_HEADS), k_norm_b), positions)
    vb = heads(vb, B_HEADS)
    outs, lses = [], []
    for g, (window, dil) in enumerate(B_PATTERNS):
        sl = slice(g * B_HEADS_PER_GROUP, (g + 1) * B_HEADS_PER_GROUP)
        o, l = dilated_window_attention(qb[:, :, sl], kb[:, :, sl], vb[:, :, sl], window, dil)
        outs.append(o)
        lses.append(l)
    alpha = jax.nn.softmax(jnp.stack(lses, axis=0), axis=0)
    yb = jnp.sum(alpha[..., None] * jnp.stack(outs, axis=0).astype(jnp.float32), axis=0)
    yb = yb.astype(h.dtype).reshape(B, S, B_OUT_W)

    gates = jax.nn.sigmoid(h @ w_gate + b_gate).reshape(B, S, 2, D_MODEL)
    merged = gates[:, :, 0] * (ya @ w_branch_a) + gates[:, :, 1] * (yb @ w_branch_b)
    return merged @ w_out


def peer_ffn(h, w_peer_q, peer_sub_keys, peer_u, peer_v):
    B, S, D = h.shape
    q = (h @ w_peer_q).reshape(B, S, PEER_HEADS, 2, PEER_HALF)
    sc = jnp.einsum('bshcd,hckd->bshck', q, peer_sub_keys,
                    preferred_element_type=jnp.float32)
    s_top, i_top = lax.top_k(sc, PEER_TOPK)
    cand = s_top[..., 0, :, None] + s_top[..., 1, None, :]
    cand_idx = i_top[..., 0, :, None] * PEER_N_KEYS + i_top[..., 1, None, :]
    cand = cand.reshape(B, S, PEER_HEADS, PEER_TOPK * PEER_TOPK)
    cand_idx = cand_idx.reshape(B, S, PEER_HEADS, PEER_TOPK * PEER_TOPK)
    best, pos = lax.top_k(cand, PEER_TOPK)
    experts = jnp.take_along_axis(cand_idx, pos, axis=-1)
    gates = jax.nn.softmax(best, axis=-1).astype(h.dtype)

    T = B * S
    nblk = T // PEER_TOKEN_BLOCK
    hf = h.reshape(nblk, PEER_TOKEN_BLOCK, D)
    ef = experts.reshape(nblk, PEER_TOKEN_BLOCK, PEER_HEADS * PEER_TOPK)
    gf = gates.reshape(nblk, PEER_TOKEN_BLOCK, PEER_HEADS * PEER_TOPK)

    def block(args):
        hb, eb, gb = args
        u = peer_u[eb]
        act = jax.nn.gelu(jnp.einsum('tkd,td->tk', u, hb), approximate=False)
        v = peer_v[eb]
        return jnp.einsum('tk,tkd->td', act * gb, v)

    y = lax.map(block, (hf, ef, gf))
    return y.reshape(B, S, D)


def setup_inputs(seed: int = 0) -> dict:
    key = jax.random.key(seed)
    ks = jax.random.split(key, 18)
    f32 = jnp.float32

    def nrm(k, shape, scale):
        return jax.random.normal(k, shape, f32) * scale

    def gain(k, n):
        return 1.0 + 0.02 * jax.random.normal(k, (n,), f32)

    return {
        "x": nrm(ks[0], (BATCH, SEQ, D_MODEL), 1.0),
        "positions": jnp.broadcast_to(jnp.arange(SEQ, dtype=jnp.int32)[None, :], (BATCH, SEQ)),
        "norm_mix": gain(ks[1], D_MODEL),
        "w_in": nrm(ks[2], (D_MODEL, IN_W), D_MODEL ** -0.5),
        "q_norm_a": gain(ks[3], HEAD_DIM),
        "k_norm_a": gain(ks[4], HEAD_DIM),
        "sinks_a": nrm(ks[5], (A_Q_HEADS,), 0.5),
        "q_norm_b": gain(ks[6], HEAD_DIM),
        "k_norm_b": gain(ks[7], HEAD_DIM),
        "w_branch_a": nrm(ks[8], (A_Q_W, D_MODEL), A_Q_W ** -0.5),
        "w_branch_b": nrm(ks[9], (B_OUT_W, D_MODEL), B_OUT_W ** -0.5),
        "w_gate": nrm(ks[10], (D_MODEL, 2 * D_MODEL), D_MODEL ** -0.5),
        "b_gate": nrm(ks[11], (2 * D_MODEL,), 0.01),
        "w_out": nrm(ks[12], (D_MODEL, D_MODEL), D_MODEL ** -0.5),
        "norm_ffn": gain(ks[13], D_MODEL),
        "w_peer_q": nrm(ks[14], (D_MODEL, PEER_HEADS * PEER_QUERY_DIM), D_MODEL ** -0.5),
        "peer_sub_keys": nrm(ks[15], (PEER_HEADS, 2, PEER_N_KEYS, PEER_HALF), PEER_HALF ** -0.5),
        "peer_u": nrm(ks[16], (PEER_N_EXPERTS, D_MODEL), D_MODEL ** -0.5),
        "peer_v": nrm(ks[17], (PEER_N_EXPERTS, D_MODEL), 0.5),
    }


def reference(x, positions, norm_mix, w_in, q_norm_a, k_norm_a, sinks_a, q_norm_b, k_norm_b,
              w_branch_a, w_branch_b, w_gate, b_gate, w_out, norm_ffn, w_peer_q, peer_sub_keys,
              peer_u, peer_v):
    for _ in range(DEPTH):
        x = x + token_mixing(rmsnorm(x, norm_mix), positions, w_in, q_norm_a, k_norm_a, sinks_a,
                             q_norm_b, k_norm_b, w_branch_a, w_branch_b, w_gate, b_gate, w_out)
        x = x + peer_ffn(rmsnorm(x, norm_ffn), w_peer_q, peer_sub_keys, peer_u, peer_v)
    return x
```

```python
import functools
import math

import jax
import jax.numpy as jnp
from jax import lax
from jax.experimental import pallas as pl
from jax.experimental.pallas import tpu as pltpu

F32 = jnp.float32
BF16 = jnp.bfloat16

LANE = 128
V7X_SCOPED_VMEM_BYTES = 60000 * 1024
COMPILER_TEMP_BYTES = 12 * 1024 * 1024

HEAD_DIM = 128
A_Q_HEADS = 16
A_KV_HEADS = 4
A_GROUP = A_Q_HEADS // A_KV_HEADS
A_WINDOW = 128
B_PATTERNS = ((128, 1), (512, 4), (2048, 16))
B_HEADS_PER_GROUP = 4
B_HEADS = B_HEADS_PER_GROUP * len(B_PATTERNS)
ROPE_THETA = 500000.0
ROPE_DIMS = HEAD_DIM // 4
ROPE_HALF = ROPE_DIMS // 2
BLOCK = 128
EPS = 1e-6
PEER_HEADS = 8
PEER_N_KEYS = 128
PEER_QUERY_DIM = 256
PEER_HALF = PEER_QUERY_DIM // 2
PEER_TOPK = 16

A_Q_W = A_Q_HEADS * HEAD_DIM
A_KV_W = A_KV_HEADS * HEAD_DIM
B_W = B_HEADS * HEAD_DIM
B_OUT_W = B_HEADS_PER_GROUP * HEAD_DIM
IN_W = A_Q_W + 2 * A_KV_W + 3 * B_W
IN_BLOCKS = IN_W // HEAD_DIM
QA_BLK, KA_BLK, VA_BLK = 0, A_Q_HEADS, A_Q_HEADS + A_KV_HEADS
QB_BLK = A_Q_HEADS + 2 * A_KV_HEADS
KB_BLK = QB_BLK + B_HEADS
VB_BLK = KB_BLK + B_HEADS

PROJ_TN = 512
PEER_PAIRS = tuple((a, b) for a in range(PEER_TOPK) for b in range(PEER_TOPK)
                   if (a + 1) * (b + 1) <= PEER_TOPK)
PEER_CAND_ROWS = -(-len(PEER_PAIRS) // 8) * 8


def _tile(n, pref):
    return pref if n % pref == 0 else n


def _nbytes(shape, dtype):
    return math.prod(shape) * jnp.dtype(dtype).itemsize


def _params(semantics, pipelined=(), resident=()):
    need = 2 * sum(_nbytes(s, d) for s, d in pipelined) + sum(_nbytes(s, d) for s, d in resident)
    limit = min(need + COMPILER_TEMP_BYTES, V7X_SCOPED_VMEM_BYTES)
    return pltpu.CompilerParams(dimension_semantics=semantics, vmem_limit_bytes=limit)


def _rope_table_kernel(pos_ref, cos_ref, sin_lo_ref, sin_hi_ref):
    pos = pos_ref[...].astype(F32)
    lane = lax.broadcasted_iota(jnp.int32, (1, LANE), 1)
    k = (lane & (ROPE_HALF - 1)).astype(F32)
    inv_freq = jnp.exp(-math.log(ROPE_THETA) * k / ROPE_HALF)
    ang = pos * inv_freq
    cos, sin = jnp.cos(ang), jnp.sin(ang)
    cos_ref[...] = jnp.where(lane < ROPE_DIMS, cos, 1.0)
    sin_lo_ref[...] = jnp.where(lane < ROPE_HALF, -sin, 0.0)
    sin_hi_ref[...] = jnp.where(lane < ROPE_DIMS, jnp.where(lane >= ROPE_HALF, sin, 0.0), 0.0)


def _rope_tables(positions):
    s = positions.shape[0]
    tm = _tile(s, 1024)
    out = jax.ShapeDtypeStruct((s, LANE), F32)
    spec = pl.BlockSpec((tm, LANE), lambda i: (i, 0))
    return pl.pallas_call(
        _rope_table_kernel, name="rope_tables",
        grid=(s // tm,),
        in_specs=[pl.BlockSpec((tm, 1), lambda i: (i, 0))],
        out_specs=[spec, spec, spec], out_shape=[out, out, out],
        compiler_params=_params(("parallel",), pipelined=[((tm, LANE), F32)] * 4),
    )(positions)


def _apply_rope(x, cos, sin_lo, sin_hi):
    return (x * cos + pltpu.roll(x, LANE - ROPE_HALF, 1) * sin_lo
            + pltpu.roll(x, ROPE_HALF, 1) * sin_hi)


def _rmsnorm_kernel(x_ref, g_ref, o_ref):
    x = x_ref[...].astype(F32)
    inv = lax.rsqrt(jnp.mean(x * x, axis=-1, keepdims=True) + EPS)
    o_ref[...] = (x * inv * g_ref[...]).astype(o_ref.dtype)


def _rmsnorm(x, gain):
    s, d = x.shape
    tm = _tile(s, 256)
    return pl.pallas_call(
        _rmsnorm_kernel, name="rmsnorm",
        grid=(s // tm,),
        in_specs=[pl.BlockSpec((tm, d), lambda i: (i, 0)), pl.BlockSpec((1, d), lambda i: (0, 0))],
        out_specs=pl.BlockSpec((tm, d), lambda i: (i, 0)),
        out_shape=jax.ShapeDtypeStruct((s, d), BF16),
        compiler_params=_params(("parallel",), pipelined=[((tm, d), F32), ((tm, d), BF16)]),
    )(x, gain.reshape(1, d))


def _proj_kernel(h_ref, w_ref, gain_ref, cos_ref, slo_ref, shi_ref, o_ref, acc_ref, *, row_chunk):
    j = pl.program_id(1)
    acc_ref[...] = jnp.dot(h_ref[...], w_ref[...], preferred_element_type=F32)
    tiles_per = lambda blk: blk * HEAD_DIM // PROJ_TN
    is_v = ((j >= tiles_per(VA_BLK)) & (j < tiles_per(QB_BLK))) | (j >= tiles_per(VB_BLK))

    @pl.when(is_v)
    def _():
        o_ref[...] = acc_ref[...].astype(o_ref.dtype)

    @pl.when(jnp.logical_not(is_v))
    def _():
        def body(r, carry):
            rows = pl.ds(pl.multiple_of(r * row_chunk, row_chunk), row_chunk)
            cos, slo, shi = cos_ref[rows, :], slo_ref[rows, :], shi_ref[rows, :]
            for hh in range(PROJ_TN // HEAD_DIM):
                cols = slice(hh * HEAD_DIM, (hh + 1) * HEAD_DIM)
                x = acc_ref[rows, cols]
                inv = lax.rsqrt(jnp.mean(x * x, axis=-1, keepdims=True) + EPS)
                xn = x * inv * gain_ref[:, cols]
                o_ref[rows, cols] = _apply_rope(xn, cos, slo, shi).astype(o_ref.dtype)
            return carry
        lax.fori_loop(0, acc_ref.shape[0] // row_chunk, body, 0)


def _in_projection(h, w_in, head_gain, cos, slo, shi):
    s, d = h.shape
    tm = _tile(s, 1024)
    row_chunk = _tile(tm, 128)
    tn = PROJ_TN
    tab = pl.BlockSpec((tm, LANE), lambda i, j: (i, 0))
    return pl.pallas_call(
        functools.partial(_proj_kernel, row_chunk=row_chunk), name="in_projection",
        grid=(s // tm, IN_W // tn),
        in_specs=[pl.BlockSpec((tm, d), lambda i, j: (i, 0)),
                  pl.BlockSpec((d, tn), lambda i, j: (0, j)),
                  pl.BlockSpec((1, tn), lambda i, j: (0, j)),
                  tab, tab, tab],
        out_specs=pl.BlockSpec((tm, tn), lambda i, j: (i, j)),
        out_shape=jax.ShapeDtypeStruct((s, IN_W), BF16),
        scratch_shapes=[pltpu.VMEM((tm, tn), F32)],
        compiler_params=_params(
            ("parallel", "arbitrary"),
            pipelined=[((tm, d), BF16), ((d, tn), BF16), ((tm, tn), BF16)] + [((tm, LANE), F32)] * 3,
            resident=[((tm, tn), F32)]),
    )(h, w_in, head_gain, cos, slo, shi)


def _gate_kernel(h_ref, w_ref, b_ref, o_ref):
    acc = jnp.dot(h_ref[...], w_ref[...], preferred_element_type=F32) + b_ref[...]
    o_ref[...] = jax.nn.sigmoid(acc).astype(o_ref.dtype)


def _gate_projection(h, w_gate, b_gate):
    s, d = h.shape
    n = w_gate.shape[1]
    tm, tn = _tile(s, 1024), _tile(n, 512)
    return pl.pallas_call(
        _gate_kernel, name="gate_projection",
        grid=(s // tm, n // tn),
        in_specs=[pl.BlockSpec((tm, d), lambda i, j: (i, 0)),
                  pl.BlockSpec((d, tn), lambda i, j: (0, j)),
                  pl.BlockSpec((1, tn), lambda i, j: (0, j))],
        out_specs=pl.BlockSpec((tm, tn), lambda i, j: (i, j)),
        out_shape=jax.ShapeDtypeStruct((s, n), BF16),
        compiler_params=_params(
            ("parallel", "arbitrary"),
            pipelined=[((tm, d), BF16), ((d, tn), BF16), ((tm, tn), BF16)],
            resident=[((tm, tn), F32)]),
    )(h, w_gate, b_gate.reshape(1, n))


def _band_attn_kernel(*refs, n_group, max_dist, has_sink, with_lse, tq, tile_axis):
    refs = list(refs)
    sink_ref = refs.pop(0) if has_sink else None
    q_ref, k_ref, kp_ref, v_ref, vp_ref, o_ref = refs[:6]
    lse_ref = refs[6] if with_lse else None
    head0 = pl.program_id(0) * n_group
    tile = pl.program_id(tile_axis)
    scale = HEAD_DIM ** -0.5

    k_ext = jnp.concatenate([kp_ref[...], k_ref[...]], axis=0)
    v_ext = jnp.concatenate([vp_ref[...], v_ref[...]], axis=0)
    qi = lax.broadcasted_iota(jnp.int32, (BLOCK, 2 * BLOCK), 0)
    kj = lax.broadcasted_iota(jnp.int32, (BLOCK, 2 * BLOCK), 1)
    dist = BLOCK + qi - kj
    band = (dist >= 0) & (dist <= max_dist)
    first_mask = band & ((kj >= BLOCK) | (tile > 0))

    for b in range(tq // BLOCK):
        rows = slice(b * BLOCK, (b + 1) * BLOCK)
        kw = k_ext[b * BLOCK:(b + 2) * BLOCK]
        vw = v_ext[b * BLOCK:(b + 2) * BLOCK]
        mask = first_mask if b == 0 else band
        for g in range(n_group):
            cols = slice(g * HEAD_DIM, (g + 1) * HEAD_DIM)
            s = lax.dot_general(q_ref[rows, cols], kw, (((1,), (1,)), ((), ())),
                                preferred_element_type=F32) * scale
            s = jnp.where(mask, s, -jnp.inf)
            m = jnp.max(s, axis=-1, keepdims=True)
            if has_sink:
                sink = sink_ref[head0 + g]
                m = jnp.maximum(m, sink)
            p = jnp.exp(s - m)
            denom = jnp.sum(p, axis=-1, keepdims=True)
            if has_sink:
                denom = denom + jnp.exp(sink - m)
            o = jnp.dot(p.astype(vw.dtype), vw, preferred_element_type=F32) * (1.0 / denom)
            o_ref[rows, cols] = o.astype(o_ref.dtype)
            if with_lse:
                lse_ref[rows, cols] = jnp.broadcast_to(m + jnp.log(denom), (BLOCK, HEAD_DIM))


def _prev_block(i, tq):
    return jnp.maximum(i * (tq // BLOCK) - 1, 0)


def _attention_a(proj, sinks):
    s = proj.shape[0]
    tq = _tile(s, 512)
    gw = A_GROUP * HEAD_DIM
    cur = lambda blk: pl.BlockSpec((tq, HEAD_DIM), lambda kv, i: (i, blk + kv))
    prev = lambda blk: pl.BlockSpec((BLOCK, HEAD_DIM), lambda kv, i: (_prev_block(i, tq), blk + kv))
    kern = functools.partial(_band_attn_kernel, n_group=A_GROUP, max_dist=A_WINDOW - 1,
                             has_sink=True, with_lse=False, tq=tq, tile_axis=1)
    return pl.pallas_call(
        kern, name="attention_a",
        grid=(A_KV_HEADS, s // tq),
        in_specs=[pl.BlockSpec(memory_space=pltpu.SMEM),
                  pl.BlockSpec((tq, gw), lambda kv, i: (i, kv)),
                  cur(KA_BLK), prev(KA_BLK), cur(VA_BLK), prev(VA_BLK)],
        out_specs=pl.BlockSpec((tq, gw), lambda kv, i: (i, kv)),
        out_shape=jax.ShapeDtypeStruct((s, A_Q_W), BF16),
        compiler_params=_params(
            ("parallel", "arbitrary"),
            pipelined=[((tq, gw), BF16)] * 2 + [((tq, HEAD_DIM), BF16)] * 2 + [((BLOCK, HEAD_DIM), BF16)] * 2),
    )(sinks, proj, proj, proj, proj, proj)


def _attention_b_group(proj, group, dilation):
    s = proj.shape[0]
    sub_len = s // dilation
    view = proj.reshape(sub_len, dilation * IN_W)
    tq = _tile(sub_len, 512)
    h0 = group * B_HEADS_PER_GROUP
    col = lambda blk: (lambda r, hh, i: (i, r * IN_BLOCKS + blk + h0 + hh))
    pcol = lambda blk: (lambda r, hh, i: (_prev_block(i, tq), r * IN_BLOCKS + blk + h0 + hh))
    cur = lambda blk: pl.BlockSpec((tq, HEAD_DIM), col(blk))
    prev = lambda blk: pl.BlockSpec((BLOCK, HEAD_DIM), pcol(blk))
    out_spec = pl.BlockSpec((tq, HEAD_DIM), lambda r, hh, i: (i, r * B_HEADS_PER_GROUP + hh))
    window = B_PATTERNS[group][0]
    kern = functools.partial(_band_attn_kernel, n_group=1, max_dist=window // dilation,
                             has_sink=False, with_lse=True, tq=tq, tile_axis=2)
    out, lse = pl.pallas_call(
        kern, name=f"attention_b{group}",
        grid=(dilation, B_HEADS_PER_GROUP, sub_len // tq),
        in_specs=[cur(QB_BLK), cur(KB_BLK), prev(KB_BLK), cur(VB_BLK), prev(VB_BLK)],
        out_specs=[out_spec, out_spec],
        out_shape=[jax.ShapeDtypeStruct((sub_len, dilation * B_OUT_W), BF16),
                   jax.ShapeDtypeStruct((sub_len, dilation * B_OUT_W), F32)],
        compiler_params=_params(
            ("parallel", "parallel", "arbitrary"),
            pipelined=[((tq, HEAD_DIM), BF16)] * 4 + [((BLOCK, HEAD_DIM), BF16)] * 2 + [((tq, HEAD_DIM), F32)]),
    )(view, view, view, view, view)
    return out.reshape(s, B_OUT_W), lse.reshape(s, B_OUT_W)


def _merge_kernel(ya_ref, o0_ref, o1_ref, o2_ref, l0_ref, l1_ref, l2_ref,
                  wa_ref, wb_ref, ga_ref, gb_ref, out_ref, yb_ref):
    @pl.when(pl.program_id(1) == 0)
    def _():
        ls = [l0_ref[...], l1_ref[...], l2_ref[...]]
        m = jnp.maximum(jnp.maximum(ls[0], ls[1]), ls[2])
        es = [jnp.exp(l - m) for l in ls]
        inv = 1.0 / (es[0] + es[1] + es[2])
        os = [o0_ref[...].astype(F32), o1_ref[...].astype(F32), o2_ref[...].astype(F32)]
        yb = (es[0] * inv) * os[0] + (es[1] * inv) * os[1] + (es[2] * inv) * os[2]
        yb_ref[...] = yb.astype(yb_ref.dtype)

    a = jnp.dot(ya_ref[...], wa_ref[...], preferred_element_type=F32)
    b = jnp.dot(yb_ref[...], wb_ref[...], preferred_element_type=F32)
    out = ga_ref[...].astype(F32) * a + gb_ref[...].astype(F32) * b
    out_ref[...] = out.astype(out_ref.dtype)


def _merge_branches(ya, outs, lses, wa, wb, gates):
    s = ya.shape[0]
    d = wa.shape[1]
    tm, tn = _tile(s, 512), _tile(d, 1024)
    nj = d // tn
    row = lambda w: pl.BlockSpec((tm, w), lambda i, j: (i, 0))
    return pl.pallas_call(
        _merge_kernel, name="merge_branches",
        grid=(s // tm, nj),
        in_specs=[row(A_Q_W)] + [row(B_OUT_W)] * 6 + [
            pl.BlockSpec((A_Q_W, tn), lambda i, j: (0, j)),
            pl.BlockSpec((B_OUT_W, tn), lambda i, j: (0, j)),
            pl.BlockSpec((tm, tn), lambda i, j: (i, j)),
            pl.BlockSpec((tm, tn), lambda i, j: (i, nj + j))],
        out_specs=pl.BlockSpec((tm, tn), lambda i, j: (i, j)),
        out_shape=jax.ShapeDtypeStruct((s, d), BF16),
        scratch_shapes=[pltpu.VMEM((tm, B_OUT_W), BF16)],
        compiler_params=_params(
            ("parallel", "arbitrary"),
            pipelined=[((tm, A_Q_W), BF16)] + [((tm, B_OUT_W), BF16)] * 3 + [((tm, B_OUT_W), F32)] * 3
            + [((A_Q_W, tn), BF16), ((B_OUT_W, tn), BF16)] + [((tm, tn), BF16)] * 3,
            resident=[((tm, B_OUT_W), BF16), ((tm, tn), F32), ((tm, tn), F32)]),
    )(ya, *outs, *lses, wa, wb, gates, gates)


def _out_proj_kernel(m_ref, w_ref, x_ref, o_ref):
    o_ref[...] = x_ref[...] + jnp.dot(m_ref[...], w_ref[...], preferred_element_type=F32)


def _out_projection(merged, w_out, x):
    s, d = x.shape
    tm, tn = _tile(s, 512), _tile(d, 1024)
    return pl.pallas_call(
        _out_proj_kernel, name="out_projection",
        grid=(s // tm, d // tn),
        in_specs=[pl.BlockSpec((tm, d), lambda i, j: (i, 0)),
                  pl.BlockSpec((d, tn), lambda i, j: (0, j)),
                  pl.BlockSpec((tm, tn), lambda i, j: (i, j))],
        out_specs=pl.BlockSpec((tm, tn), lambda i, j: (i, j)),
        out_shape=jax.ShapeDtypeStruct((s, d), F32),
        compiler_params=_params(
            ("parallel", "arbitrary"),
            pipelined=[((tm, d), BF16), ((d, tn), BF16), ((tm, tn), F32), ((tm, tn), F32)]),
    )(merged, w_out, x)


def _peer_score_kernel(h_ref, wq_ref, keys_ref, sc0_ref, sc1_ref):
    q = jnp.dot(h_ref[...], wq_ref[...], preferred_element_type=F32).astype(BF16)
    for c, sc_ref in enumerate((sc0_ref, sc1_ref)):
        qc = q[:, c * PEER_HALF:(c + 1) * PEER_HALF]
        sc_ref[0] = lax.dot_general(keys_ref[0, c], qc, (((1,), (1,)), ((), ())),
                                    preferred_element_type=F32)


def _peer_scores(h, wq, sub_keys):
    s, d = h.shape
    tm = _tile(s, 512)
    out_spec = pl.BlockSpec((1, PEER_N_KEYS, tm), lambda i, hd: (hd, 0, i))
    out_shape = jax.ShapeDtypeStruct((PEER_HEADS, PEER_N_KEYS, s), F32)
    return pl.pallas_call(
        _peer_score_kernel, name="peer_scores",
        grid=(s // tm, PEER_HEADS),
        in_specs=[pl.BlockSpec((tm, d), lambda i, hd: (i, 0)),
                  pl.BlockSpec((d, PEER_QUERY_DIM), lambda i, hd: (0, hd)),
                  pl.BlockSpec((1, 2, PEER_N_KEYS, PEER_HALF), lambda i, hd: (hd, 0, 0, 0))],
        out_specs=[out_spec, out_spec], out_shape=[out_shape, out_shape],
        compiler_params=_params(
            ("parallel", "arbitrary"),
            pipelined=[((tm, d), BF16), ((d, PEER_QUERY_DIM), BF16), ((2, PEER_N_KEYS, tm), F32)]),
    )(h, wq, sub_keys)


def _top_rows(x, k):
    n = x.shape[0]
    idx = lax.broadcasted_iota(jnp.int32, x.shape, 0)
    rows = []
    for _ in range(k):
        m = jnp.max(x, axis=0, keepdims=True)
        first = jnp.min(jnp.where(x == m, idx, n), axis=0, keepdims=True)
        x = jnp.where(idx == first, -jnp.inf, x)
        rows.append(m)
    return rows


def _peer_route_kernel(sc0_ref, sc1_ref, e0_ref, e1_ref, tau_ref, cand_ref):
    n_chunks = sc0_ref.shape[-1] // LANE
    pad_rows = PEER_CAND_ROWS - len(PEER_PAIRS)
    if pad_rows:
        cand_ref[len(PEER_PAIRS):, :] = jnp.full((pad_rows, LANE), -jnp.inf, F32)

    def body(c, carry):
        lanes = pl.ds(pl.multiple_of(c * LANE, LANE), LANE)
        s0 = sc0_ref[0, :, lanes]
        s1 = sc1_ref[0, :, lanes]
        t0 = _top_rows(s0, PEER_TOPK)
        t1 = _top_rows(s1, PEER_TOPK)
        for r, (a, b) in enumerate(PEER_PAIRS):
            cand_ref[r:r + 1, :] = t0[a] + t1[b]
        best = _top_rows(cand_ref[...], PEER_TOPK)
        z = jnp.ones_like(best[0])
        for bk in best[1:]:
            z = z + jnp.exp(bk - best[0])
        e0_ref[0, :, lanes] = jnp.exp(s0 - t0[0]) * (1.0 / z)
        e1_ref[0, :, lanes] = jnp.exp(s1 - t1[0])
        tau_ref[0, :, lanes] = best[PEER_TOPK - 1]
        return carry
    lax.fori_loop(0, n_chunks, body, 0)


def _peer_route(sc0, sc1):
    s = sc0.shape[-1]
    tl = _tile(s, 512)
    fac = pl.BlockSpec((1, PEER_N_KEYS, tl), lambda hd, i: (hd, 0, i))
    fac_shape = jax.ShapeDtypeStruct((PEER_HEADS, PEER_N_KEYS, s), F32)
    return pl.pallas_call(
        _peer_route_kernel, name="peer_route",
        grid=(PEER_HEADS, s // tl),
        in_specs=[fac, fac],
        out_specs=[fac, fac, pl.BlockSpec((1, 1, tl), lambda hd, i: (hd, 0, i))],
        out_shape=[fac_shape, fac_shape, jax.ShapeDtypeStruct((PEER_HEADS, 1, s), F32)],
        scratch_shapes=[pltpu.VMEM((PEER_CAND_ROWS, LANE), F32)],
        compiler_params=_params(
            ("parallel", "arbitrary"),
            pipelined=[((PEER_N_KEYS, tl), F32)] * 5,
            resident=[((PEER_CAND_ROWS, LANE), F32)]),
    )(sc0, sc1)


def _peer_expert_kernel(h_ref, u_ref, v_ref, sc0_ref, e0_ref, sc1_ref, e1_ref, tau_ref, y_ref,
                        s_ref, p_ref):
    te, tm = s_ref.shape
    ni = te // PEER_N_KEYS

    @pl.when(pl.program_id(1) == 0)
    def _():
        y_ref[...] = jnp.zeros_like(y_ref)

    s_ref[...] = lax.dot_general(u_ref[...], h_ref[...], (((1,), (1,)), ((), ())),
                                 preferred_element_type=F32)
    sqrt_half = math.sqrt(0.5)
    for ii in range(ni):
        rows = slice(ii * PEER_N_KEYS, (ii + 1) * PEER_N_KEYS)
        for tc in range(tm // LANE):
            lanes = slice(tc * LANE, (tc + 1) * LANE)
            gate = jnp.zeros((PEER_N_KEYS, LANE), F32)
            for hd in range(PEER_HEADS):
                pair = sc0_ref[hd, ii:ii + 1, lanes] + sc1_ref[hd, :, lanes]
                w = e0_ref[hd, ii:ii + 1, lanes] * e1_ref[hd, :, lanes]
                gate = gate + jnp.where(pair >= tau_ref[hd, :, lanes], w, 0.0)
            z = s_ref[rows, lanes]
            act = 0.5 * z * (1.0 + lax.erf(z * sqrt_half))
            p_ref[lanes, rows] = (act * gate).T.astype(p_ref.dtype)
    y_ref[...] += jnp.dot(p_ref[...], v_ref[...], preferred_element_type=F32)


def _peer_experts(h, u, v, sc0, sc1, e0, e1, tau):
    s, d = h.shape
    n_exp = u.shape[0]
    tm, te = _tile(s, 512), 512
    ni = te // PEER_N_KEYS
    single = dict(pipeline_mode=pl.Buffered(1))
    by_tile = lambda a: a.reshape(PEER_HEADS, PEER_N_KEYS // ni, ni, s)
    tile_rows = pl.BlockSpec((PEER_HEADS, None, ni, tm), lambda t, e: (0, e, 0, t))
    fac = pl.BlockSpec((PEER_HEADS, PEER_N_KEYS, tm), lambda t, e: (0, 0, t), **single)
    return pl.pallas_call(
        _peer_expert_kernel, name="peer_experts",
        grid=(s // tm, n_exp // te),
        in_specs=[pl.BlockSpec((tm, d), lambda t, e: (t, 0), **single),
                  pl.BlockSpec((te, d), lambda t, e: (e, 0)),
                  pl.BlockSpec((te, d), lambda t, e: (e, 0)),
                  tile_rows, tile_rows, fac, fac,
                  pl.BlockSpec((PEER_HEADS, 1, tm), lambda t, e: (0, 0, t), **single)],
        out_specs=pl.BlockSpec((tm, d), lambda t, e: (t, 0)),
        out_shape=jax.ShapeDtypeStruct((s, d), F32),
        scratch_shapes=[pltpu.VMEM((te, tm), F32), pltpu.VMEM((tm, te), BF16)],
        compiler_params=_params(
            ("parallel", "arbitrary"),
            pipelined=[((te, d), BF16)] * 2 + [((tm, d), F32)] + [((PEER_HEADS, 8, tm), F32)] * 2,
            resident=[((tm, d), BF16), ((PEER_HEADS, 2, PEER_N_KEYS, tm), F32),
                      ((te, tm), F32), ((tm, te), BF16)]),
    )(h, u, v, by_tile(sc0), by_tile(e0), sc1, e1, tau)


def _add_kernel(a_ref, b_ref, o_ref):
    o_ref[...] = a_ref[...] + b_ref[...]


def _residual_add(a, b):
    s, d = a.shape
    tm = _tile(s, 256)
    spec = pl.BlockSpec((tm, d), lambda i: (i, 0))
    return pl.pallas_call(
        _add_kernel, name="residual_add",
        grid=(s // tm,), in_specs=[spec, spec], out_specs=spec,
        out_shape=jax.ShapeDtypeStruct((s, d), F32),
        compiler_params=_params(("parallel",), pipelined=[((tm, d), F32)] * 3),
    )(a, b)


def _head_gain_row(q_norm_a, k_norm_a, q_norm_b, k_norm_b):
    ones = jnp.ones((HEAD_DIM,), F32)
    segs = ([q_norm_a] * A_Q_HEADS + [k_norm_a] * A_KV_HEADS + [ones] * A_KV_HEADS
            + [q_norm_b] * B_HEADS + [k_norm_b] * B_HEADS + [ones] * B_HEADS)
    return jnp.concatenate([g.astype(F32) for g in segs]).reshape(1, IN_W)


def _layer(x, positions, norm_mix, w_in, q_norm_a, k_norm_a, sinks_a, q_norm_b, k_norm_b,
           w_branch_a, w_branch_b, w_gate, b_gate, w_out, norm_ffn, w_peer_q, peer_sub_keys,
           peer_u, peer_v):
    cos, slo, shi = _rope_tables(positions.reshape(-1, 1))
    h = _rmsnorm(x, norm_mix)
    proj = _in_projection(h, w_in.astype(BF16), _head_gain_row(q_norm_a, k_norm_a, q_norm_b, k_norm_b),
                          cos, slo, shi)
    gates = _gate_projection(h, w_gate.astype(BF16), b_gate.astype(F32))
    ya = _attention_a(proj, sinks_a.astype(F32))
    outs, lses = zip(*[_attention_b_group(proj, g, dil) for g, (_, dil) in enumerate(B_PATTERNS)])
    merged = _merge_branches(ya, outs, lses, w_branch_a.astype(BF16), w_branch_b.astype(BF16), gates)
    x = _out_projection(merged, w_out.astype(BF16), x)

    h = _rmsnorm(x, norm_ffn)
    sc0, sc1 = _peer_scores(h, w_peer_q.astype(BF16), peer_sub_keys.astype(BF16))
    e0, e1, tau = _peer_route(sc0, sc1)
    y = _peer_experts(h, peer_u.astype(BF16), peer_v.astype(BF16), sc0, sc1, e0, e1, tau)
    return _residual_add(x, y)


def kernel(x, positions, norm_mix, w_in, q_norm_a, k_norm_a, sinks_a, q_norm_b, k_norm_b, w_branch_a,
           w_branch_b, w_gate, b_gate, w_out, norm_ffn, w_peer_q, peer_sub_keys, peer_u, peer_v):
    batch, seq, d_model = x.shape
    assert batch == 1, "kernels are written for a single sequence"
    out = _layer(x.reshape(seq, d_model), positions.reshape(seq), norm_mix, w_in, q_norm_a, k_norm_a,
                 sinks_a, q_norm_b, k_norm_b, w_branch_a, w_branch_b, w_gate, b_gate, w_out, norm_ffn,
                 w_peer_q, peer_sub_keys, peer_u, peer_v)
    return out.reshape(batch, seq, d_model)
```

```python
import functools
import math

import jax
import jax.numpy as jnp
from jax import lax
from jax.experimental import pallas as pl
from jax.experimental.pallas import tpu as pltpu

F32 = jnp.float32
BF16 = jnp.bfloat16

LANE = 128
V7X_SCOPED_VMEM_BYTES = 60000 * 1024
COMPILER_TEMP_BYTES = 12 * 1024 * 1024

HEAD_DIM = 128
A_Q_HEADS = 16
A_KV_HEADS = 4
A_GROUP = A_Q_HEADS // A_KV_HEADS
A_WINDOW = 128
B_PATTERNS = ((128, 1), (512, 4), (2048, 16))
B_HEADS_PER_GROUP = 4
B_HEADS = B_HEADS_PER_GROUP * len(B_PATTERNS)
ROPE_THETA = 500000.0
ROPE_DIMS = HEAD_DIM // 4
ROPE_HALF = ROPE_DIMS // 2
BLOCK = 128
EPS = 1e-6
PEER_HEADS = 8
PEER_N_KEYS = 128
PEER_QUERY_DIM = 256
PEER_HALF = PEER_QUERY_DIM // 2
PEER_TOPK = 16

A_Q_W = A_Q_HEADS * HEAD_DIM
A_KV_W = A_KV_HEADS * HEAD_DIM
B_W = B_HEADS * HEAD_DIM
B_OUT_W = B_HEADS_PER_GROUP * HEAD_DIM
IN_W = A_Q_W + 2 * A_KV_W + 3 * B_W
IN_BLOCKS = IN_W // HEAD_DIM
QA_BLK, KA_BLK, VA_BLK = 0, A_Q_HEADS, A_Q_HEADS + A_KV_HEADS
QB_BLK = A_Q_HEADS + 2 * A_KV_HEADS
KB_BLK = QB_BLK + B_HEADS
VB_BLK = KB_BLK + B_HEADS

PROJ_TN = 512
PEER_PAIRS = tuple((a, b) for a in range(PEER_TOPK) for b in range(PEER_TOPK)
                   if (a + 1) * (b + 1) <= PEER_TOPK)
PEER_CAND_ROWS = -(-len(PEER_PAIRS) // 8) * 8


def _tile(n, pref):
    return pref if n % pref == 0 else n


def _nbytes(shape, dtype):
    return math.prod(shape) * jnp.dtype(dtype).itemsize


def _params(semantics, pipelined=(), resident=(), flags=None):
    need = 2 * sum(_nbytes(s, d) for s, d in pipelined) + sum(_nbytes(s, d) for s, d in resident)
    limit = min(need + COMPILER_TEMP_BYTES, V7X_SCOPED_VMEM_BYTES)
    return pltpu.CompilerParams(dimension_semantics=semantics, vmem_limit_bytes=limit, flags=flags)


def _rope_table_kernel(pos_ref, cos_ref, sin_lo_ref, sin_hi_ref):
    pos = pos_ref[...].astype(F32)
    lane = lax.broadcasted_iota(jnp.int32, (1, LANE), 1)
    k = (lane & (ROPE_HALF - 1)).astype(F32)
    inv_freq = jnp.exp(-math.log(ROPE_THETA) * k / ROPE_HALF)
    ang = pos * inv_freq
    cos, sin = jnp.cos(ang), jnp.sin(ang)
    cos_ref[...] = jnp.where(lane < ROPE_DIMS, cos, 1.0)
    sin_lo_ref[...] = jnp.where(lane < ROPE_HALF, -sin, 0.0)
    sin_hi_ref[...] = jnp.where(lane < ROPE_DIMS, jnp.where(lane >= ROPE_HALF, sin, 0.0), 0.0)


def _rope_tables(positions):
    s = positions.shape[0]
    tm = _tile(s, 1024)
    out = jax.ShapeDtypeStruct((s, LANE), F32)
    spec = pl.BlockSpec((tm, LANE), lambda i: (i, 0))
    return pl.pallas_call(
        _rope_table_kernel, name="rope_tables",
        grid=(s // tm,),
        in_specs=[pl.BlockSpec((tm, 1), lambda i: (i, 0))],
        out_specs=[spec, spec, spec], out_shape=[out, out, out],
        compiler_params=_params(("parallel",), pipelined=[((tm, LANE), F32)] * 4),
    )(positions)


def _apply_rope(x, cos, sin_lo, sin_hi):
    return (x * cos + pltpu.roll(x, LANE - ROPE_HALF, 1) * sin_lo
            + pltpu.roll(x, ROPE_HALF, 1) * sin_hi)


def _rmsnorm_kernel(x_ref, g_ref, o_ref):
    x = x_ref[...].astype(F32)
    inv = lax.rsqrt(jnp.mean(x * x, axis=-1, keepdims=True) + EPS)
    o_ref[...] = (x * inv * g_ref[...]).astype(o_ref.dtype)


def _rmsnorm(x, gain):
    s, d = x.shape
    tm = _tile(s, 256)
    return pl.pallas_call(
        _rmsnorm_kernel, name="rmsnorm",
        grid=(s // tm,),
        in_specs=[pl.BlockSpec((tm, d), lambda i: (i, 0)), pl.BlockSpec((1, d), lambda i: (0, 0))],
        out_specs=pl.BlockSpec((tm, d), lambda i: (i, 0)),
        out_shape=jax.ShapeDtypeStruct((s, d), BF16),
        compiler_params=_params(("parallel",), pipelined=[((tm, d), F32), ((tm, d), BF16)]),
    )(x, gain.reshape(1, d))


def _cast_weight_once(w_ref, wb_ref):
    @pl.when(pl.program_id(1) == 0)
    def _():
        wb_ref[...] = w_ref[...].astype(wb_ref.dtype)


def _proj_kernel(h_ref, w_ref, gain_ref, cos_ref, slo_ref, shi_ref, o_ref, wb_ref, *, row_chunk):
    _cast_weight_once(w_ref, wb_ref)
    j = pl.program_id(0)
    tiles_per = lambda blk: blk * HEAD_DIM // PROJ_TN
    is_v = ((j >= tiles_per(VA_BLK)) & (j < tiles_per(QB_BLK))) | (j >= tiles_per(VB_BLK))
    keep = jnp.where(is_v, 0.0, 1.0)
    for r in range(h_ref.shape[0] // row_chunk):
        rows = slice(r * row_chunk, (r + 1) * row_chunk)
        acc = jnp.dot(h_ref[rows, :], wb_ref[...], preferred_element_type=F32)
        cos = jnp.where(is_v, 1.0, cos_ref[rows, :])
        slo, shi = slo_ref[rows, :] * keep, shi_ref[rows, :] * keep
        for hh in range(PROJ_TN // HEAD_DIM):
            cols = slice(hh * HEAD_DIM, (hh + 1) * HEAD_DIM)
            x = acc[:, cols]
            inv = lax.rsqrt(jnp.mean(x * x, axis=-1, keepdims=True) + EPS)
            xn = x * jnp.where(is_v, 1.0, inv) * gain_ref[:, cols]
            o_ref[rows, cols] = _apply_rope(xn, cos, slo, shi).astype(o_ref.dtype)


def _in_projection(h, w_in, head_gain, cos, slo, shi):
    s, d = h.shape
    tm = _tile(s, 1024)
    row_chunk = _tile(tm, 256)
    tn = PROJ_TN
    tab = pl.BlockSpec((tm, LANE), lambda j, i: (i, 0))
    return pl.pallas_call(
        functools.partial(_proj_kernel, row_chunk=row_chunk), name="in_projection",
        grid=(IN_W // tn, s // tm),
        in_specs=[pl.BlockSpec((tm, d), lambda j, i: (i, 0)),
                  pl.BlockSpec((d, tn), lambda j, i: (0, j)),
                  pl.BlockSpec((1, tn), lambda j, i: (0, j)),
                  tab, tab, tab],
        out_specs=pl.BlockSpec((tm, tn), lambda j, i: (i, j)),
        out_shape=jax.ShapeDtypeStruct((s, IN_W), BF16),
        scratch_shapes=[pltpu.VMEM((d, tn), BF16)],
        compiler_params=_params(
            ("parallel", "arbitrary"),
            pipelined=[((tm, d), BF16), ((d, tn), F32), ((tm, tn), BF16)] + [((tm, LANE), F32)] * 3,
            resident=[((d, tn), BF16), ((row_chunk, tn), F32)]),
    )(h, w_in, head_gain, cos, slo, shi)


def _gate_kernel(h_ref, w_ref, b_ref, u_ref, v_ref, o_ref, ub_ref, vb_ref, wb_ref):
    _cast_weight_once(w_ref, wb_ref)
    acc = jnp.dot(h_ref[...], wb_ref[...], preferred_element_type=F32) + b_ref[...]
    o_ref[...] = jax.nn.sigmoid(acc).astype(o_ref.dtype)
    ub_ref[...] = u_ref[...].astype(ub_ref.dtype)
    vb_ref[...] = v_ref[...].astype(vb_ref.dtype)


def _gate_projection(h, w_gate, b_gate, peer_u, peer_v):
    s, d = h.shape
    n = w_gate.shape[1]
    tm, tn = _tile(s, 1024), _tile(n, 512)
    n_i = s // tm
    n_exp = peer_u.shape[0]
    slab = n_exp // (n_i * (n // tn))
    assert slab * n_i * (n // tn) == n_exp and slab % 16 == 0
    slab_spec = pl.BlockSpec((slab, d), lambda j, i: (j * n_i + i, 0))
    table = jax.ShapeDtypeStruct((n_exp, d), BF16)
    return pl.pallas_call(
        _gate_kernel, name="gate_projection",
        grid=(n // tn, n_i),
        in_specs=[pl.BlockSpec((tm, d), lambda j, i: (i, 0)),
                  pl.BlockSpec((d, tn), lambda j, i: (0, j)),
                  pl.BlockSpec((1, tn), lambda j, i: (0, j)),
                  slab_spec, slab_spec],
        out_specs=[pl.BlockSpec((tm, tn), lambda j, i: (i, j)), slab_spec, slab_spec],
        out_shape=[jax.ShapeDtypeStruct((s, n), BF16), table, table],
        scratch_shapes=[pltpu.VMEM((d, tn), BF16)],
        compiler_params=_params(
            ("parallel", "arbitrary"),
            pipelined=[((tm, d), BF16), ((d, tn), F32), ((tm, tn), BF16)]
            + [((slab, d), F32)] * 2 + [((slab, d), BF16)] * 2,
            resident=[((d, tn), BF16), ((tm, tn), F32)]),
    )(h, w_gate, b_gate.reshape(1, n), peer_u, peer_v)


def _band_masks(max_dist, has_prev):
    qi = lax.broadcasted_iota(jnp.int32, (BLOCK, 2 * BLOCK), 0)
    kj = lax.broadcasted_iota(jnp.int32, (BLOCK, 2 * BLOCK), 1)
    dist = BLOCK + qi - kj
    band = (dist >= 0) & (dist <= max_dist)
    return band, band & ((kj >= BLOCK) | has_prev)


def _band_block(q, kw, vw, mask, sink=None):
    s = lax.dot_general(q, kw, (((1,), (1,)), ((), ())), preferred_element_type=F32) * HEAD_DIM ** -0.5
    s = jnp.where(mask, s, -jnp.inf)
    m = jnp.max(s, axis=-1, keepdims=True)
    if sink is not None:
        m = jnp.maximum(m, sink)
    p = jnp.exp(s - m)
    denom = jnp.sum(p, axis=-1, keepdims=True)
    if sink is not None:
        denom = denom + jnp.exp(sink - m)
    o = jnp.dot(p.astype(vw.dtype), vw, preferred_element_type=F32) * (1.0 / denom)
    return o, m + jnp.log(denom)


def _dilated_attn_kernel(q_ref, k_ref, kp_ref, v_ref, vp_ref, o_ref, lse_ref,
                         qf_ref, kf_ref, vf_ref, of_ref, lf_ref, *, dilation, max_dist):
    tile_rows = q_ref.shape[0]
    qf_ref[...] = q_ref[...].astype(F32)
    kf_ref[:tile_rows, :] = kp_ref[...].astype(F32)
    kf_ref[tile_rows:, :] = k_ref[...].astype(F32)
    vf_ref[:tile_rows, :] = vp_ref[...].astype(F32)
    vf_ref[tile_rows:, :] = v_ref[...].astype(F32)
    _, mask = _band_masks(max_dist, pl.program_id(1) > 0)
    for r in range(dilation):
        q = qf_ref[pl.ds(r, BLOCK, stride=dilation), :].astype(BF16)
        kw = kf_ref[pl.ds(r, 2 * BLOCK, stride=dilation), :].astype(BF16)
        vw = vf_ref[pl.ds(r, 2 * BLOCK, stride=dilation), :].astype(BF16)
        o, lse = _band_block(q, kw, vw, mask)
        of_ref[pl.ds(r, BLOCK, stride=dilation), :] = o
        lf_ref[pl.ds(r, BLOCK, stride=dilation), :] = jnp.broadcast_to(lse, (BLOCK, HEAD_DIM))
    o_ref[...] = of_ref[...].astype(o_ref.dtype)
    lse_ref[...] = lf_ref[...]


def _attention_b_dilated(proj, group, dilation):
    s = proj.shape[0]
    rows = dilation * BLOCK
    h0 = group * B_HEADS_PER_GROUP
    cur = lambda blk: pl.BlockSpec((rows, HEAD_DIM), lambda hh, i: (i, blk + h0 + hh))
    prev = lambda blk: pl.BlockSpec((rows, HEAD_DIM), lambda hh, i: (jnp.maximum(i - 1, 0), blk + h0 + hh))
    out_spec = pl.BlockSpec((rows, HEAD_DIM), lambda hh, i: (i, hh))
    window = B_PATTERNS[group][0]
    kern = functools.partial(_dilated_attn_kernel, dilation=dilation, max_dist=window // dilation)
    f32_tile = lambda n: pltpu.VMEM((n * rows, HEAD_DIM), F32)
    return pl.pallas_call(
        kern, name=f"attention_b{group}",
        grid=(B_HEADS_PER_GROUP, s // rows),
        in_specs=[cur(QB_BLK), cur(KB_BLK), prev(KB_BLK), cur(VB_BLK), prev(VB_BLK)],
        out_specs=[out_spec, out_spec],
        out_shape=[jax.ShapeDtypeStruct((s, B_OUT_W), BF16), jax.ShapeDtypeStruct((s, B_OUT_W), F32)],
        scratch_shapes=[f32_tile(1), f32_tile(2), f32_tile(2), f32_tile(1), f32_tile(1)],
        compiler_params=_params(
            ("parallel", "arbitrary"),
            pipelined=[((rows, HEAD_DIM), BF16)] * 6 + [((rows, HEAD_DIM), F32)],
            resident=[((7 * rows, HEAD_DIM), F32)]),
    )(proj, proj, proj, proj, proj)


def _band_attn_kernel(*refs, n_group, max_dist, has_sink, with_lse, tq, tile_axis):
    refs = list(refs)
    sink_ref = refs.pop(0) if has_sink else None
    q_ref, k_ref, kp_ref, v_ref, vp_ref, o_ref = refs[:6]
    lse_ref = refs[6] if with_lse else None
    head0 = pl.program_id(0) * n_group

    k_ext = jnp.concatenate([kp_ref[...], k_ref[...]], axis=0)
    v_ext = jnp.concatenate([vp_ref[...], v_ref[...]], axis=0)
    band, first_mask = _band_masks(max_dist, pl.program_id(tile_axis) > 0)

    for b in range(tq // BLOCK):
        rows = slice(b * BLOCK, (b + 1) * BLOCK)
        kw = k_ext[b * BLOCK:(b + 2) * BLOCK]
        vw = v_ext[b * BLOCK:(b + 2) * BLOCK]
        mask = first_mask if b == 0 else band
        for g in range(n_group):
            cols = slice(g * HEAD_DIM, (g + 1) * HEAD_DIM)
            sink = sink_ref[head0 + g] if has_sink else None
            o, lse = _band_block(q_ref[rows, cols], kw, vw, mask, sink)
            o_ref[rows, cols] = o.astype(o_ref.dtype)
            if with_lse:
                lse_ref[rows, cols] = jnp.broadcast_to(lse, (BLOCK, HEAD_DIM))


def _prev_block(i, tq):
    return jnp.maximum(i * (tq // BLOCK) - 1, 0)


def _attention_a(proj, sinks):
    s = proj.shape[0]
    tq = _tile(s, 512)
    gw = A_GROUP * HEAD_DIM
    cur = lambda blk: pl.BlockSpec((tq, HEAD_DIM), lambda kv, i: (i, blk + kv))
    prev = lambda blk: pl.BlockSpec((BLOCK, HEAD_DIM), lambda kv, i: (_prev_block(i, tq), blk + kv))
    kern = functools.partial(_band_attn_kernel, n_group=A_GROUP, max_dist=A_WINDOW - 1,
                             has_sink=True, with_lse=False, tq=tq, tile_axis=1)
    return pl.pallas_call(
        kern, name="attention_a",
        grid=(A_KV_HEADS, s // tq),
        in_specs=[pl.BlockSpec(memory_space=pltpu.SMEM),
                  pl.BlockSpec((tq, gw), lambda kv, i: (i, kv)),
                  cur(KA_BLK), prev(KA_BLK), cur(VA_BLK), prev(VA_BLK)],
        out_specs=pl.BlockSpec((tq, gw), lambda kv, i: (i, kv)),
        out_shape=jax.ShapeDtypeStruct((s, A_Q_W), BF16),
        compiler_params=_params(
            ("parallel", "arbitrary"),
            pipelined=[((tq, gw), BF16)] * 2 + [((tq, HEAD_DIM), BF16)] * 2 + [((BLOCK, HEAD_DIM), BF16)] * 2),
    )(sinks, proj, proj, proj, proj, proj)


def _attention_b_group(proj, group, dilation):
    s = proj.shape[0]
    sub_len = s // dilation
    view = proj.reshape(sub_len, dilation * IN_W)
    tq = _tile(sub_len, 512)
    h0 = group * B_HEADS_PER_GROUP
    col = lambda blk: (lambda r, hh, i: (i, r * IN_BLOCKS + blk + h0 + hh))
    pcol = lambda blk: (lambda r, hh, i: (_prev_block(i, tq), r * IN_BLOCKS + blk + h0 + hh))
    cur = lambda blk: pl.BlockSpec((tq, HEAD_DIM), col(blk))
    prev = lambda blk: pl.BlockSpec((BLOCK, HEAD_DIM), pcol(blk))
    out_spec = pl.BlockSpec((tq, HEAD_DIM), lambda r, hh, i: (i, r * B_HEADS_PER_GROUP + hh))
    window = B_PATTERNS[group][0]
    kern = functools.partial(_band_attn_kernel, n_group=1, max_dist=window // dilation,
                             has_sink=False, with_lse=True, tq=tq, tile_axis=2)
    out, lse = pl.pallas_call(
        kern, name=f"attention_b{group}",
        grid=(dilation, B_HEADS_PER_GROUP, sub_len // tq),
        in_specs=[cur(QB_BLK), cur(KB_BLK), prev(KB_BLK), cur(VB_BLK), prev(VB_BLK)],
        out_specs=[out_spec, out_spec],
        out_shape=[jax.ShapeDtypeStruct((sub_len, dilation * B_OUT_W), BF16),
                   jax.ShapeDtypeStruct((sub_len, dilation * B_OUT_W), F32)],
        compiler_params=_params(
            ("parallel", "parallel", "arbitrary"),
            pipelined=[((tq, HEAD_DIM), BF16)] * 4 + [((BLOCK, HEAD_DIM), BF16)] * 2 + [((tq, HEAD_DIM), F32)]),
    )(view, view, view, view, view)
    return out.reshape(s, B_OUT_W), lse.reshape(s, B_OUT_W)


def _merge_kernel(ya_ref, o0_ref, o1_ref, o2_ref, l0_ref, l1_ref, l2_ref,
                  wa_ref, wb_ref, ga_ref, gb_ref, out_ref, yb_ref):
    @pl.when(pl.program_id(1) == 0)
    def _():
        ls = [l0_ref[...], l1_ref[...], l2_ref[...]]
        m = jnp.maximum(jnp.maximum(ls[0], ls[1]), ls[2])
        es = [jnp.exp(l - m) for l in ls]
        inv = 1.0 / (es[0] + es[1] + es[2])
        os = [o0_ref[...].astype(F32), o1_ref[...].astype(F32), o2_ref[...].astype(F32)]
        yb = (es[0] * inv) * os[0] + (es[1] * inv) * os[1] + (es[2] * inv) * os[2]
        yb_ref[...] = yb.astype(yb_ref.dtype)

    a = jnp.dot(ya_ref[...], wa_ref[...], preferred_element_type=F32)
    b = jnp.dot(yb_ref[...], wb_ref[...], preferred_element_type=F32)
    out = ga_ref[...].astype(F32) * a + gb_ref[...].astype(F32) * b
    out_ref[...] = out.astype(out_ref.dtype)


def _merge_branches(ya, outs, lses, wa, wb, gates):
    s = ya.shape[0]
    d = wa.shape[1]
    tm, tn = _tile(s, 512), _tile(d, 1024)
    nj = d // tn
    row = lambda w: pl.BlockSpec((tm, w), lambda i, j: (i, 0))
    return pl.pallas_call(
        _merge_kernel, name="merge_branches",
        grid=(s // tm, nj),
        in_specs=[row(A_Q_W)] + [row(B_OUT_W)] * 6 + [
            pl.BlockSpec((A_Q_W, tn), lambda i, j: (0, j)),
            pl.BlockSpec((B_OUT_W, tn), lambda i, j: (0, j)),
            pl.BlockSpec((tm, tn), lambda i, j: (i, j)),
            pl.BlockSpec((tm, tn), lambda i, j: (i, nj + j))],
        out_specs=pl.BlockSpec((tm, tn), lambda i, j: (i, j)),
        out_shape=jax.ShapeDtypeStruct((s, d), BF16),
        scratch_shapes=[pltpu.VMEM((tm, B_OUT_W), BF16)],
        compiler_params=_params(
            ("parallel", "arbitrary"),
            pipelined=[((tm, A_Q_W), BF16)] + [((tm, B_OUT_W), BF16)] * 3 + [((tm, B_OUT_W), F32)] * 3
            + [((A_Q_W, tn), BF16), ((B_OUT_W, tn), BF16)] + [((tm, tn), BF16)] * 3,
            resident=[((tm, B_OUT_W), BF16), ((tm, tn), F32), ((tm, tn), F32)]),
    )(ya, *outs, *lses, wa, wb, gates, gates)


def _out_proj_kernel(m_ref, w_ref, x_ref, o_ref, wb_ref):
    _cast_weight_once(w_ref, wb_ref)
    o_ref[...] = x_ref[...] + jnp.dot(m_ref[...], wb_ref[...], preferred_element_type=F32)


def _out_projection(merged, w_out, x):
    s, d = x.shape
    tm, tn = _tile(s, 1024), _tile(d, 512)
    return pl.pallas_call(
        _out_proj_kernel, name="out_projection",
        grid=(d // tn, s // tm),
        in_specs=[pl.BlockSpec((tm, d), lambda j, i: (i, 0)),
                  pl.BlockSpec((d, tn), lambda j, i: (0, j)),
                  pl.BlockSpec((tm, tn), lambda j, i: (i, j))],
        out_specs=pl.BlockSpec((tm, tn), lambda j, i: (i, j)),
        out_shape=jax.ShapeDtypeStruct((s, d), F32),
        scratch_shapes=[pltpu.VMEM((d, tn), BF16)],
        compiler_params=_params(
            ("parallel", "arbitrary"),
            pipelined=[((tm, d), BF16), ((d, tn), F32), ((tm, tn), F32), ((tm, tn), F32)],
            resident=[((d, tn), BF16)]),
    )(merged, w_out, x)


def _peer_score_kernel(h_ref, wq_ref, keys_ref, sc0_ref, sc1_ref, wb_ref):
    _cast_weight_once(wq_ref, wb_ref)
    q = jnp.dot(h_ref[...], wb_ref[...], preferred_element_type=F32).astype(BF16)
    for c, sc_ref in enumerate((sc0_ref, sc1_ref)):
        qc = q[:, c * PEER_HALF:(c + 1) * PEER_HALF]
        sc_ref[0] = lax.dot_general(keys_ref[0, c].astype(BF16), qc, (((1,), (1,)), ((), ())),
                                    preferred_element_type=F32)


def _peer_scores(h, wq, sub_keys):
    s, d = h.shape
    tm = _tile(s, 1024)
    out_spec = pl.BlockSpec((1, PEER_N_KEYS, tm), lambda hd, i: (hd, 0, i))
    out_shape = jax.ShapeDtypeStruct((PEER_HEADS, PEER_N_KEYS, s), F32)
    return pl.pallas_call(
        _peer_score_kernel, name="peer_scores",
        grid=(PEER_HEADS, s // tm),
        in_specs=[pl.BlockSpec((tm, d), lambda hd, i: (i, 0)),
                  pl.BlockSpec((d, PEER_QUERY_DIM), lambda hd, i: (0, hd)),
                  pl.BlockSpec((1, 2, PEER_N_KEYS, PEER_HALF), lambda hd, i: (hd, 0, 0, 0))],
        out_specs=[out_spec, out_spec], out_shape=[out_shape, out_shape],
        scratch_shapes=[pltpu.VMEM((d, PEER_QUERY_DIM), BF16)],
        compiler_params=_params(
            ("parallel", "arbitrary"),
            pipelined=[((tm, d), BF16), ((d, PEER_QUERY_DIM), F32), ((2, PEER_N_KEYS, tm), F32),
                       ((2, PEER_N_KEYS, PEER_HALF), F32)],
            resident=[((d, PEER_QUERY_DIM), BF16)]),
    )(h, wq, sub_keys)


def _top_rows(x, k):
    n = x.shape[0]
    idx = lax.broadcasted_iota(jnp.int32, x.shape, 0)
    rows = []
    for _ in range(k):
        m = jnp.max(x, axis=0, keepdims=True)
        first = jnp.min(jnp.where(x == m, idx, n), axis=0, keepdims=True)
        x = jnp.where(idx == first, -jnp.inf, x)
        rows.append(m)
    return rows


def _top_rows_pair(xa, xb, k):
    n = xa.shape[0]
    idx = lax.broadcasted_iota(jnp.int32, xa.shape, 0)
    rows_a, rows_b = [], []
    for _ in range(k):
        ma = jnp.max(xa, axis=0, keepdims=True)
        mb = jnp.max(xb, axis=0, keepdims=True)
        fa = jnp.min(jnp.where(xa == ma, idx, n), axis=0, keepdims=True)
        fb = jnp.min(jnp.where(xb == mb, idx, n), axis=0, keepdims=True)
        xa = jnp.where(idx == fa, -jnp.inf, xa)
        xb = jnp.where(idx == fb, -jnp.inf, xb)
        rows_a.append(ma)
        rows_b.append(mb)
    return rows_a, rows_b


def _peer_route_kernel(sc0_ref, sc1_ref, e0_ref, e1_ref, tau_ref, cand_ref):
    n_chunks = sc0_ref.shape[-1] // LANE
    pad_rows = PEER_CAND_ROWS - len(PEER_PAIRS)
    if pad_rows:
        cand_ref[:, len(PEER_PAIRS):, :] = jnp.full((2, pad_rows, LANE), -jnp.inf, F32)

    def chunk(c, cand):
        lanes = pl.ds(pl.multiple_of(c * LANE, LANE), LANE)
        s0 = sc0_ref[0, :, lanes]
        s1 = sc1_ref[0, :, lanes]
        t0, t1 = _top_rows_pair(s0, s1, PEER_TOPK)
        for r, (a, b) in enumerate(PEER_PAIRS):
            cand[r:r + 1, :] = t0[a] + t1[b]
        best = _top_rows(cand[...], PEER_TOPK)
        z = jnp.ones_like(best[0])
        for bk in best[1:]:
            z = z + jnp.exp(bk - best[0])
        e0_ref[0, :, lanes] = jnp.exp(s0 - t0[0]) * (1.0 / z)
        e1_ref[0, :, lanes] = jnp.exp(s1 - t1[0])
        tau_ref[0, :, lanes] = best[PEER_TOPK - 1]

    def body(c, carry):
        chunk(2 * c, cand_ref.at[0])
        chunk(2 * c + 1, cand_ref.at[1])
        return carry
    lax.fori_loop(0, n_chunks // 2, body, 0)


def _peer_route(sc0, sc1):
    s = sc0.shape[-1]
    tl = _tile(s, 512)
    fac = pl.BlockSpec((1, PEER_N_KEYS, tl), lambda hd, i: (hd, 0, i))
    fac_shape = jax.ShapeDtypeStruct((PEER_HEADS, PEER_N_KEYS, s), F32)
    return pl.pallas_call(
        _peer_route_kernel, name="peer_route",
        grid=(PEER_HEADS, s // tl),
        in_specs=[fac, fac],
        out_specs=[fac, fac, pl.BlockSpec((1, 1, tl), lambda hd, i: (hd, 0, i))],
        out_shape=[fac_shape, fac_shape, jax.ShapeDtypeStruct((PEER_HEADS, 1, s), F32)],
        scratch_shapes=[pltpu.VMEM((2, PEER_CAND_ROWS, LANE), F32)],
        compiler_params=_params(
            ("parallel", "arbitrary"),
            pipelined=[((PEER_N_KEYS, tl), F32)] * 5,
            resident=[((2, PEER_CAND_ROWS, LANE), F32)]),
    )(sc0, sc1)


def _slot_row(ref, hd, i, lanes, slot, n_slots):
    per_slot = ref.shape[1] // n_slots
    row = ref[hd, i:i + 1, lanes]
    for q in range(1, n_slots):
        row = jnp.where(slot == q, ref[hd, q * per_slot + i:q * per_slot + i + 1, lanes], row)
    return row


def _peer_gate_block(s_ref, p_ref, sc0_ref, e0_ref, i, slot, n_slots, ii, lane0, sc1_ref, e1_ref, tau_ref):
    rows = slice(ii * PEER_N_KEYS, (ii + 1) * PEER_N_KEYS)
    lanes = pl.ds(lane0, LANE)
    gate = jnp.zeros((PEER_N_KEYS, LANE), F32)
    for hd in range(PEER_HEADS):
        pair = _slot_row(sc0_ref, hd, i, lanes, slot, n_slots) + sc1_ref[hd, :, lanes]
        w = _slot_row(e0_ref, hd, i, lanes, slot, n_slots) * e1_ref[hd, :, lanes]
        gate = gate + jnp.where(pair >= tau_ref[hd, :, lanes], w, 0.0)
    z = s_ref[rows, lanes]
    act = 0.5 * z * (1.0 + lax.erf(z * math.sqrt(0.5)))
    p_ref[lanes, rows] = (act * gate).T.astype(p_ref.dtype)


def _peer_expert_kernel(h_ref, u_ref, v_ref, sc0p_ref, e0p_ref, sc0c_ref, e0c_ref, sc1_ref, e1_ref,
                        tau_ref, x_ref, y_ref, s0_ref, s1_ref, p0_ref, p1_ref, ht_ref, *, n_pairs):
    te, tm = s0_ref.shape
    ni = te // PEER_N_KEYS
    n_tc = tm // LANE
    n_slots = sc0c_ref.shape[1] // (2 * ni)
    k = pl.program_id(1)

    @pl.when(k == 0)
    def _():
        y_ref[...] = x_ref[...]
        s1_ref[...] = jnp.zeros_like(s1_ref)
        p0_ref[...] = jnp.zeros_like(p0_ref)
        ht_ref[...] = h_ref[...].T

    def half(u_row0, s_new, s_old, p_new, p_old, sc0_ref, e0_ref, pair, row0):
        slot = lax.rem(pair, n_slots)
        y_ref[...] += jnp.dot(p_old[...], v_ref[u_row0:u_row0 + te, :], preferred_element_type=F32)
        for ii in range(ni):
            for tc in range(n_tc):
                _peer_gate_block(s_old, p_new, sc0_ref, e0_ref, row0 + ii, slot, n_slots, ii, tc * LANE,
                                 sc1_ref, e1_ref, tau_ref)
        s_new[...] = jnp.dot(u_ref[u_row0:u_row0 + te, :], ht_ref[...], preferred_element_type=F32)

    half(0, s0_ref, s1_ref, p1_ref, p0_ref, sc0p_ref, e0p_ref, jnp.maximum(k - 1, 0), ni)
    half(te, s1_ref, s0_ref, p0_ref, p1_ref, sc0c_ref, e0c_ref, jnp.minimum(k, n_pairs - 1), 0)


def _peer_experts(h, u, v, sc0, sc1, e0, e1, tau, x):
    s, d = h.shape
    n_exp = u.shape[0]
    tm, te = _tile(s, 512), 256
    n_pairs = n_exp // (2 * te)
    rows_per_pair = 2 * te // PEER_N_KEYS
    single = dict(pipeline_mode=pl.Buffered(1))
    cur = lambda k: jnp.minimum(k, n_pairs - 1)
    prev = lambda k: jnp.maximum(k - 1, 0)
    block_rows = max(rows_per_pair, 8)
    n_slots = block_rows // rows_per_pair
    pair_rows = lambda idx: pl.BlockSpec((PEER_HEADS, block_rows, tm),
                                         lambda t, k: (0, idx(k) // n_slots, t))
    fac = pl.BlockSpec((PEER_HEADS, PEER_N_KEYS, tm), lambda t, k: (0, 0, t), **single)
    return pl.pallas_call(
        functools.partial(_peer_expert_kernel, n_pairs=n_pairs), name="peer_experts",
        grid=(s // tm, n_pairs + 1),
        in_specs=[pl.BlockSpec((tm, d), lambda t, k: (t, 0), **single),
                  pl.BlockSpec((2 * te, d), lambda t, k: (cur(k), 0)),
                  pl.BlockSpec((2 * te, d), lambda t, k: (prev(k), 0)),
                  pair_rows(prev), pair_rows(prev), pair_rows(cur), pair_rows(cur),
                  fac, fac,
                  pl.BlockSpec((PEER_HEADS, 1, tm), lambda t, k: (0, 0, t), **single),
                  pl.BlockSpec((tm, d), lambda t, k: (t, 0), **single)],
        out_specs=pl.BlockSpec((tm, d), lambda t, k: (t, 0)),
        out_shape=jax.ShapeDtypeStruct((s, d), F32),
        scratch_shapes=[pltpu.VMEM((te, tm), F32), pltpu.VMEM((te, tm), F32),
                        pltpu.VMEM((tm, te), BF16), pltpu.VMEM((tm, te), BF16),
                        pltpu.VMEM((d, tm), BF16)],
        compiler_params=_params(
            ("parallel", "arbitrary"),
            pipelined=[((2 * te, d), BF16)] * 2 + [((tm, d), F32)] + [((PEER_HEADS, 8, tm), F32)] * 4,
            resident=[((tm, d), BF16), ((tm, d), BF16), ((tm, d), F32),
                      ((PEER_HEADS, 2, PEER_N_KEYS, tm), F32), ((2 * te, tm), F32), ((tm, 2 * te), BF16)],
            ),
    )(h, u, v, sc0, e0, sc0, e0, sc1, e1, tau, x)


def _head_gain_row(q_norm_a, k_norm_a, q_norm_b, k_norm_b):
    ones = jnp.ones((HEAD_DIM,), F32)
    segs = ([q_norm_a] * A_Q_HEADS + [k_norm_a] * A_KV_HEADS + [ones] * A_KV_HEADS
            + [q_norm_b] * B_HEADS + [k_norm_b] * B_HEADS + [ones] * B_HEADS)
    return jnp.concatenate([g.astype(F32) for g in segs]).reshape(1, IN_W)


def _layer(x, positions, norm_mix, w_in, q_norm_a, k_norm_a, sinks_a, q_norm_b, k_norm_b,
           w_branch_a, w_branch_b, w_gate, b_gate, w_out, norm_ffn, w_peer_q, peer_sub_keys,
           peer_u, peer_v):
    cos, slo, shi = _rope_tables(positions.reshape(-1, 1))
    h = _rmsnorm(x, norm_mix)
    proj = _in_projection(h, w_in, _head_gain_row(q_norm_a, k_norm_a, q_norm_b, k_norm_b), cos, slo, shi)
    gates, peer_u16, peer_v16 = _gate_projection(h, w_gate, b_gate.astype(F32), peer_u, peer_v)
    ya = _attention_a(proj, sinks_a.astype(F32))
    outs, lses = zip(*[(_attention_b_group if dil == 1 else _attention_b_dilated)(proj, g, dil)
                       for g, (_, dil) in enumerate(B_PATTERNS)])
    merged = _merge_branches(ya, outs, lses, w_branch_a.astype(BF16), w_branch_b.astype(BF16), gates)
    x = _out_projection(merged, w_out, x)

    h = _rmsnorm(x, norm_ffn)
    sc0, sc1 = _peer_scores(h, w_peer_q, peer_sub_keys)
    e0, e1, tau = _peer_route(sc0, sc1)
    return _peer_experts(h, peer_u16, peer_v16, sc0, sc1, e0, e1, tau, x)


def kernel(x, positions, norm_mix, w_in, q_norm_a, k_norm_a, sinks_a, q_norm_b, k_norm_b, w_branch_a,
           w_branch_b, w_gate, b_gate, w_out, norm_ffn, w_peer_q, peer_sub_keys, peer_u, peer_v):
    batch, seq, d_model = x.shape
    assert batch == 1, "kernels are written for a single sequence"
    out = _layer(x.reshape(seq, d_model), positions.reshape(seq), norm_mix, w_in, q_norm_a, k_norm_a,
                 sinks_a, q_norm_b, k_norm_b, w_branch_a, w_branch_b, w_gate, b_gate, w_out, norm_ffn,
                 w_peer_q, peer_sub_keys, peer_u, peer_v)
    return out.reshape(batch, seq, d_model)
```

```python
import functools
import math

import jax
import jax.numpy as jnp
from jax import lax
from jax.experimental import pallas as pl
from jax.experimental.pallas import tpu as pltpu

F32 = jnp.float32
BF16 = jnp.bfloat16

LANE = 128
V7X_SCOPED_VMEM_BYTES = 60000 * 1024
COMPILER_TEMP_BYTES = 12 * 1024 * 1024

HEAD_DIM = 128
A_Q_HEADS = 16
A_KV_HEADS = 4
A_GROUP = A_Q_HEADS // A_KV_HEADS
A_WINDOW = 128
B_PATTERNS = ((128, 1), (512, 4), (2048, 16))
B_HEADS_PER_GROUP = 4
B_HEADS = B_HEADS_PER_GROUP * len(B_PATTERNS)
ROPE_THETA = 500000.0
ROPE_DIMS = HEAD_DIM // 4
ROPE_HALF = ROPE_DIMS // 2
BLOCK = 128
EPS = 1e-6
PEER_HEADS = 8
PEER_N_KEYS = 128
PEER_QUERY_DIM = 256
PEER_HALF = PEER_QUERY_DIM // 2
PEER_TOPK = 16

A_Q_W = A_Q_HEADS * HEAD_DIM
A_KV_W = A_KV_HEADS * HEAD_DIM
B_W = B_HEADS * HEAD_DIM
B_OUT_W = B_HEADS_PER_GROUP * HEAD_DIM
IN_W = A_Q_W + 2 * A_KV_W + 3 * B_W
IN_BLOCKS = IN_W // HEAD_DIM
QA_BLK, KA_BLK, VA_BLK = 0, A_Q_HEADS, A_Q_HEADS + A_KV_HEADS
QB_BLK = A_Q_HEADS + 2 * A_KV_HEADS
KB_BLK = QB_BLK + B_HEADS
VB_BLK = KB_BLK + B_HEADS

PROJ_TN = 512
PEER_PAIRS = tuple((a, b) for a in range(PEER_TOPK) for b in range(PEER_TOPK)
                   if (a + 1) * (b + 1) <= PEER_TOPK)
PEER_CAND_ROWS = -(-len(PEER_PAIRS) // 8) * 8


def _tile(n, pref):
    return pref if n % pref == 0 else n


def _nbytes(shape, dtype):
    return math.prod(shape) * jnp.dtype(dtype).itemsize


def _params(semantics, pipelined=(), resident=(), flags=None):
    need = 2 * sum(_nbytes(s, d) for s, d in pipelined) + sum(_nbytes(s, d) for s, d in resident)
    limit = min(need + COMPILER_TEMP_BYTES, V7X_SCOPED_VMEM_BYTES)
    return pltpu.CompilerParams(dimension_semantics=semantics, vmem_limit_bytes=limit, flags=flags)


def _rope_table_kernel(pos_ref, cos_ref, sin_lo_ref, sin_hi_ref):
    pos = pos_ref[...].astype(F32)
    lane = lax.broadcasted_iota(jnp.int32, (1, LANE), 1)
    k = (lane & (ROPE_HALF - 1)).astype(F32)
    inv_freq = jnp.exp(-math.log(ROPE_THETA) * k / ROPE_HALF)
    ang = pos * inv_freq
    cos, sin = jnp.cos(ang), jnp.sin(ang)
    cos_ref[...] = jnp.where(lane < ROPE_DIMS, cos, 1.0)
    sin_lo_ref[...] = jnp.where(lane < ROPE_HALF, -sin, 0.0)
    sin_hi_ref[...] = jnp.where(lane < ROPE_DIMS, jnp.where(lane >= ROPE_HALF, sin, 0.0), 0.0)


def _rope_tables(positions):
    s = positions.shape[0]
    tm = _tile(s, 1024)
    out = jax.ShapeDtypeStruct((s, LANE), F32)
    spec = pl.BlockSpec((tm, LANE), lambda i: (i, 0))
    return pl.pallas_call(
        _rope_table_kernel, name="rope_tables",
        grid=(s // tm,),
        in_specs=[pl.BlockSpec((tm, 1), lambda i: (i, 0))],
        out_specs=[spec, spec, spec], out_shape=[out, out, out],
        compiler_params=_params(("parallel",), pipelined=[((tm, LANE), F32)] * 4),
    )(positions)


def _apply_rope(x, cos, sin_lo, sin_hi):
    return (x * cos + pltpu.roll(x, LANE - ROPE_HALF, 1) * sin_lo
            + pltpu.roll(x, ROPE_HALF, 1) * sin_hi)


def _rmsnorm_kernel(x_ref, g_ref, o_ref):
    x = x_ref[...].astype(F32)
    inv = lax.rsqrt(jnp.mean(x * x, axis=-1, keepdims=True) + EPS)
    o_ref[...] = (x * inv * g_ref[...]).astype(o_ref.dtype)


def _rmsnorm(x, gain):
    s, d = x.shape
    tm = _tile(s, 256)
    return pl.pallas_call(
        _rmsnorm_kernel, name="rmsnorm",
        grid=(s // tm,),
        in_specs=[pl.BlockSpec((tm, d), lambda i: (i, 0)), pl.BlockSpec((1, d), lambda i: (0, 0))],
        out_specs=pl.BlockSpec((tm, d), lambda i: (i, 0)),
        out_shape=jax.ShapeDtypeStruct((s, d), BF16),
        compiler_params=_params(("parallel",), pipelined=[((tm, d), F32), ((tm, d), BF16)]),
    )(x, gain.reshape(1, d))


def _cast_weight_once(w_ref, wb_ref):
    @pl.when(pl.program_id(1) == 0)
    def _():
        wb_ref[...] = w_ref[...].astype(wb_ref.dtype)


def _proj_kernel(h_ref, w_ref, gain_ref, cos_ref, slo_ref, shi_ref, o_ref, wb_ref, *, row_chunk):
    _cast_weight_once(w_ref, wb_ref)
    j = pl.program_id(0)
    tiles_per = lambda blk: blk * HEAD_DIM // PROJ_TN
    is_v = ((j >= tiles_per(VA_BLK)) & (j < tiles_per(QB_BLK))) | (j >= tiles_per(VB_BLK))
    keep = jnp.where(is_v, 0.0, 1.0)
    for r in range(h_ref.shape[0] // row_chunk):
        rows = slice(r * row_chunk, (r + 1) * row_chunk)
        acc = jnp.dot(h_ref[rows, :], wb_ref[...], preferred_element_type=F32)
        cos = jnp.where(is_v, 1.0, cos_ref[rows, :])
        slo, shi = slo_ref[rows, :] * keep, shi_ref[rows, :] * keep
        for hh in range(PROJ_TN // HEAD_DIM):
            cols = slice(hh * HEAD_DIM, (hh + 1) * HEAD_DIM)
            x = acc[:, cols]
            inv = lax.rsqrt(jnp.mean(x * x, axis=-1, keepdims=True) + EPS)
            xn = x * jnp.where(is_v, 1.0, inv) * gain_ref[:, cols]
            o_ref[rows, cols] = _apply_rope(xn, cos, slo, shi).astype(o_ref.dtype)


def _in_projection(h, w_in, head_gain, cos, slo, shi):
    s, d = h.shape
    tm = _tile(s, 1024)
    row_chunk = _tile(tm, 256)
    tn = PROJ_TN
    tab = pl.BlockSpec((tm, LANE), lambda j, i: (i, 0))
    return pl.pallas_call(
        functools.partial(_proj_kernel, row_chunk=row_chunk), name="in_projection",
        grid=(IN_W // tn, s // tm),
        in_specs=[pl.BlockSpec((tm, d), lambda j, i: (i, 0)),
                  pl.BlockSpec((d, tn), lambda j, i: (0, j)),
                  pl.BlockSpec((1, tn), lambda j, i: (0, j)),
                  tab, tab, tab],
        out_specs=pl.BlockSpec((tm, tn), lambda j, i: (i, j)),
        out_shape=jax.ShapeDtypeStruct((s, IN_W), BF16),
        scratch_shapes=[pltpu.VMEM((d, tn), BF16)],
        compiler_params=_params(
            ("parallel", "arbitrary"),
            pipelined=[((tm, d), BF16), ((d, tn), F32), ((tm, tn), BF16)] + [((tm, LANE), F32)] * 3,
            resident=[((d, tn), BF16), ((row_chunk, tn), F32)]),
    )(h, w_in, head_gain, cos, slo, shi)


def _gate_kernel(h_ref, w_ref, b_ref, u_ref, v_ref, o_ref, ub_ref, vb_ref, wb_ref):
    _cast_weight_once(w_ref, wb_ref)
    acc = jnp.dot(h_ref[...], wb_ref[...], preferred_element_type=F32) + b_ref[...]
    o_ref[...] = jax.nn.sigmoid(acc).astype(o_ref.dtype)
    ub_ref[...] = u_ref[...].astype(ub_ref.dtype)
    vb_ref[...] = v_ref[...].astype(vb_ref.dtype)


def _gate_projection(h, w_gate, b_gate, peer_u, peer_v):
    s, d = h.shape
    n = w_gate.shape[1]
    tm, tn = _tile(s, 1024), _tile(n, 512)
    n_i = s // tm
    n_exp = peer_u.shape[0]
    slab = n_exp // (n_i * (n // tn))
    assert slab * n_i * (n // tn) == n_exp and slab % 16 == 0
    slab_spec = pl.BlockSpec((slab, d), lambda j, i: (j * n_i + i, 0))
    table = jax.ShapeDtypeStruct((n_exp, d), BF16)
    return pl.pallas_call(
        _gate_kernel, name="gate_projection",
        grid=(n // tn, n_i),
        in_specs=[pl.BlockSpec((tm, d), lambda j, i: (i, 0)),
                  pl.BlockSpec((d, tn), lambda j, i: (0, j)),
                  pl.BlockSpec((1, tn), lambda j, i: (0, j)),
                  slab_spec, slab_spec],
        out_specs=[pl.BlockSpec((tm, tn), lambda j, i: (i, j)), slab_spec, slab_spec],
        out_shape=[jax.ShapeDtypeStruct((s, n), BF16), table, table],
        scratch_shapes=[pltpu.VMEM((d, tn), BF16)],
        compiler_params=_params(
            ("parallel", "arbitrary"),
            pipelined=[((tm, d), BF16), ((d, tn), F32), ((tm, tn), BF16)]
            + [((slab, d), F32)] * 2 + [((slab, d), BF16)] * 2,
            resident=[((d, tn), BF16), ((tm, tn), F32)]),
    )(h, w_gate, b_gate.reshape(1, n), peer_u, peer_v)


def _band_masks(max_dist, has_prev):
    qi = lax.broadcasted_iota(jnp.int32, (BLOCK, 2 * BLOCK), 0)
    kj = lax.broadcasted_iota(jnp.int32, (BLOCK, 2 * BLOCK), 1)
    dist = BLOCK + qi - kj
    band = (dist >= 0) & (dist <= max_dist)
    return band, band & ((kj >= BLOCK) | has_prev)


def _band_block(q, kw, vw, mask, sink=None):
    s = lax.dot_general(q, kw, (((1,), (1,)), ((), ())), preferred_element_type=F32) * HEAD_DIM ** -0.5
    s = jnp.where(mask, s, -jnp.inf)
    m = jnp.max(s, axis=-1, keepdims=True)
    if sink is not None:
        m = jnp.maximum(m, sink)
    p = jnp.exp(s - m)
    denom = jnp.sum(p, axis=-1, keepdims=True)
    if sink is not None:
        denom = denom + jnp.exp(sink - m)
    o = jnp.dot(p.astype(vw.dtype), vw, preferred_element_type=F32) * (1.0 / denom)
    return o, m + jnp.log(denom)


def _dilated_attn_kernel(q_ref, k_ref, kp_ref, v_ref, vp_ref, o_ref, lse_ref,
                         qf_ref, kf_ref, vf_ref, of_ref, lf_ref, *, dilation, max_dist):
    tile_rows = q_ref.shape[0]
    qf_ref[...] = q_ref[...].astype(F32)
    kf_ref[:tile_rows, :] = kp_ref[...].astype(F32)
    kf_ref[tile_rows:, :] = k_ref[...].astype(F32)
    vf_ref[:tile_rows, :] = vp_ref[...].astype(F32)
    vf_ref[tile_rows:, :] = v_ref[...].astype(F32)
    _, mask = _band_masks(max_dist, pl.program_id(1) > 0)
    for r in range(dilation):
        q = qf_ref[pl.ds(r, BLOCK, stride=dilation), :].astype(BF16)
        kw = kf_ref[pl.ds(r, 2 * BLOCK, stride=dilation), :].astype(BF16)
        vw = vf_ref[pl.ds(r, 2 * BLOCK, stride=dilation), :].astype(BF16)
        o, lse = _band_block(q, kw, vw, mask)
        of_ref[pl.ds(r, BLOCK, stride=dilation), :] = o
        lf_ref[pl.ds(r, BLOCK, stride=dilation), :] = jnp.broadcast_to(lse, (BLOCK, HEAD_DIM))
    o_ref[...] = of_ref[...].astype(o_ref.dtype)
    lse_ref[...] = lf_ref[...]


def _attention_b_dilated(proj, group, dilation):
    s = proj.shape[0]
    rows = dilation * BLOCK
    h0 = group * B_HEADS_PER_GROUP
    cur = lambda blk: pl.BlockSpec((rows, HEAD_DIM), lambda hh, i: (i, blk + h0 + hh))
    prev = lambda blk: pl.BlockSpec((rows, HEAD_DIM), lambda hh, i: (jnp.maximum(i - 1, 0), blk + h0 + hh))
    out_spec = pl.BlockSpec((rows, HEAD_DIM), lambda hh, i: (i, hh))
    window = B_PATTERNS[group][0]
    kern = functools.partial(_dilated_attn_kernel, dilation=dilation, max_dist=window // dilation)
    f32_tile = lambda n: pltpu.VMEM((n * rows, HEAD_DIM), F32)
    return pl.pallas_call(
        kern, name=f"attention_b{group}",
        grid=(B_HEADS_PER_GROUP, s // rows),
        in_specs=[cur(QB_BLK), cur(KB_BLK), prev(KB_BLK), cur(VB_BLK), prev(VB_BLK)],
        out_specs=[out_spec, out_spec],
        out_shape=[jax.ShapeDtypeStruct((s, B_OUT_W), BF16), jax.ShapeDtypeStruct((s, B_OUT_W), F32)],
        scratch_shapes=[f32_tile(1), f32_tile(2), f32_tile(2), f32_tile(1), f32_tile(1)],
        compiler_params=_params(
            ("parallel", "arbitrary"),
            pipelined=[((rows, HEAD_DIM), BF16)] * 6 + [((rows, HEAD_DIM), F32)],
            resident=[((7 * rows, HEAD_DIM), F32)]),
    )(proj, proj, proj, proj, proj)


def _band_attn_kernel(*refs, n_group, max_dist, has_sink, with_lse, tq, tile_axis):
    refs = list(refs)
    sink_ref = refs.pop(0) if has_sink else None
    q_ref, k_ref, kp_ref, v_ref, vp_ref, o_ref = refs[:6]
    lse_ref = refs[6] if with_lse else None
    head0 = pl.program_id(0) * n_group

    k_ext = jnp.concatenate([kp_ref[...], k_ref[...]], axis=0)
    v_ext = jnp.concatenate([vp_ref[...], v_ref[...]], axis=0)
    band, first_mask = _band_masks(max_dist, pl.program_id(tile_axis) > 0)

    for b in range(tq // BLOCK):
        rows = slice(b * BLOCK, (b + 1) * BLOCK)
        kw = k_ext[b * BLOCK:(b + 2) * BLOCK]
        vw = v_ext[b * BLOCK:(b + 2) * BLOCK]
        mask = first_mask if b == 0 else band
        for g in range(n_group):
            cols = slice(g * HEAD_DIM, (g + 1) * HEAD_DIM)
            sink = sink_ref[head0 + g] if has_sink else None
            o, lse = _band_block(q_ref[rows, cols], kw, vw, mask, sink)
            o_ref[rows, cols] = o.astype(o_ref.dtype)
            if with_lse:
                lse_ref[rows, cols] = jnp.broadcast_to(lse, (BLOCK, HEAD_DIM))


def _prev_block(i, tq):
    return jnp.maximum(i * (tq // BLOCK) - 1, 0)


def _attention_a(proj, sinks):
    s = proj.shape[0]
    tq = _tile(s, 512)
    gw = A_GROUP * HEAD_DIM
    cur = lambda blk: pl.BlockSpec((tq, HEAD_DIM), lambda kv, i: (i, blk + kv))
    prev = lambda blk: pl.BlockSpec((BLOCK, HEAD_DIM), lambda kv, i: (_prev_block(i, tq), blk + kv))
    kern = functools.partial(_band_attn_kernel, n_group=A_GROUP, max_dist=A_WINDOW - 1,
                             has_sink=True, with_lse=False, tq=tq, tile_axis=1)
    return pl.pallas_call(
        kern, name="attention_a",
        grid=(A_KV_HEADS, s // tq),
        in_specs=[pl.BlockSpec(memory_space=pltpu.SMEM),
                  pl.BlockSpec((tq, gw), lambda kv, i: (i, kv)),
                  cur(KA_BLK), prev(KA_BLK), cur(VA_BLK), prev(VA_BLK)],
        out_specs=pl.BlockSpec((tq, gw), lambda kv, i: (i, kv)),
        out_shape=jax.ShapeDtypeStruct((s, A_Q_W), BF16),
        compiler_params=_params(
            ("parallel", "arbitrary"),
            pipelined=[((tq, gw), BF16)] * 2 + [((tq, HEAD_DIM), BF16)] * 2 + [((BLOCK, HEAD_DIM), BF16)] * 2),
    )(sinks, proj, proj, proj, proj, proj)


def _attention_b_group(proj, group, dilation):
    s = proj.shape[0]
    sub_len = s // dilation
    view = proj.reshape(sub_len, dilation * IN_W)
    tq = _tile(sub_len, 512)
    h0 = group * B_HEADS_PER_GROUP
    col = lambda blk: (lambda r, hh, i: (i, r * IN_BLOCKS + blk + h0 + hh))
    pcol = lambda blk: (lambda r, hh, i: (_prev_block(i, tq), r * IN_BLOCKS + blk + h0 + hh))
    cur = lambda blk: pl.BlockSpec((tq, HEAD_DIM), col(blk))
    prev = lambda blk: pl.BlockSpec((BLOCK, HEAD_DIM), pcol(blk))
    out_spec = pl.BlockSpec((tq, HEAD_DIM), lambda r, hh, i: (i, r * B_HEADS_PER_GROUP + hh))
    window = B_PATTERNS[group][0]
    kern = functools.partial(_band_attn_kernel, n_group=1, max_dist=window // dilation,
                             has_sink=False, with_lse=True, tq=tq, tile_axis=2)
    out, lse = pl.pallas_call(
        kern, name=f"attention_b{group}",
        grid=(dilation, B_HEADS_PER_GROUP, sub_len // tq),
        in_specs=[cur(QB_BLK), cur(KB_BLK), prev(KB_BLK), cur(VB_BLK), prev(VB_BLK)],
        out_specs=[out_spec, out_spec],
        out_shape=[jax.ShapeDtypeStruct((sub_len, dilation * B_OUT_W), BF16),
                   jax.ShapeDtypeStruct((sub_len, dilation * B_OUT_W), F32)],
        compiler_params=_params(
            ("parallel", "parallel", "arbitrary"),
            pipelined=[((tq, HEAD_DIM), BF16)] * 4 + [((BLOCK, HEAD_DIM), BF16)] * 2 + [((tq, HEAD_DIM), F32)]),
    )(view, view, view, view, view)
    return out.reshape(s, B_OUT_W), lse.reshape(s, B_OUT_W)


def _merge_kernel(ya_ref, o0_ref, o1_ref, o2_ref, l0_ref, l1_ref, l2_ref,
                  wa_ref, wb_ref, ga_ref, gb_ref, out_ref, yb_ref):
    @pl.when(pl.program_id(1) == 0)
    def _():
        ls = [l0_ref[...], l1_ref[...], l2_ref[...]]
        m = jnp.maximum(jnp.maximum(ls[0], ls[1]), ls[2])
        es = [jnp.exp(l - m) for l in ls]
        inv = 1.0 / (es[0] + es[1] + es[2])
        os = [o0_ref[...].astype(F32), o1_ref[...].astype(F32), o2_ref[...].astype(F32)]
        yb = (es[0] * inv) * os[0] + (es[1] * inv) * os[1] + (es[2] * inv) * os[2]
        yb_ref[...] = yb.astype(yb_ref.dtype)

    a = jnp.dot(ya_ref[...], wa_ref[...], preferred_element_type=F32)
    b = jnp.dot(yb_ref[...], wb_ref[...], preferred_element_type=F32)
    out = ga_ref[...].astype(F32) * a + gb_ref[...].astype(F32) * b
    out_ref[...] = out.astype(out_ref.dtype)


def _merge_branches(ya, outs, lses, wa, wb, gates):
    s = ya.shape[0]
    d = wa.shape[1]
    tm, tn = _tile(s, 512), _tile(d, 1024)
    nj = d // tn
    row = lambda w: pl.BlockSpec((tm, w), lambda i, j: (i, 0))
    return pl.pallas_call(
        _merge_kernel, name="merge_branches",
        grid=(s // tm, nj),
        in_specs=[row(A_Q_W)] + [row(B_OUT_W)] * 6 + [
            pl.BlockSpec((A_Q_W, tn), lambda i, j: (0, j)),
            pl.BlockSpec((B_OUT_W, tn), lambda i, j: (0, j)),
            pl.BlockSpec((tm, tn), lambda i, j: (i, j)),
            pl.BlockSpec((tm, tn), lambda i, j: (i, nj + j))],
        out_specs=pl.BlockSpec((tm, tn), lambda i, j: (i, j)),
        out_shape=jax.ShapeDtypeStruct((s, d), BF16),
        scratch_shapes=[pltpu.VMEM((tm, B_OUT_W), BF16)],
        compiler_params=_params(
            ("parallel", "arbitrary"),
            pipelined=[((tm, A_Q_W), BF16)] + [((tm, B_OUT_W), BF16)] * 3 + [((tm, B_OUT_W), F32)] * 3
            + [((A_Q_W, tn), BF16), ((B_OUT_W, tn), BF16)] + [((tm, tn), BF16)] * 3,
            resident=[((tm, B_OUT_W), BF16), ((tm, tn), F32), ((tm, tn), F32)]),
    )(ya, *outs, *lses, wa, wb, gates, gates)


def _out_proj_kernel(m_ref, w_ref, x_ref, o_ref, wb_ref):
    _cast_weight_once(w_ref, wb_ref)
    o_ref[...] = x_ref[...] + jnp.dot(m_ref[...], wb_ref[...], preferred_element_type=F32)


def _out_projection(merged, w_out, x):
    s, d = x.shape
    tm, tn = _tile(s, 1024), _tile(d, 512)
    return pl.pallas_call(
        _out_proj_kernel, name="out_projection",
        grid=(d // tn, s // tm),
        in_specs=[pl.BlockSpec((tm, d), lambda j, i: (i, 0)),
                  pl.BlockSpec((d, tn), lambda j, i: (0, j)),
                  pl.BlockSpec((tm, tn), lambda j, i: (i, j))],
        out_specs=pl.BlockSpec((tm, tn), lambda j, i: (i, j)),
        out_shape=jax.ShapeDtypeStruct((s, d), F32),
        scratch_shapes=[pltpu.VMEM((d, tn), BF16)],
        compiler_params=_params(
            ("parallel", "arbitrary"),
            pipelined=[((tm, d), BF16), ((d, tn), F32), ((tm, tn), F32), ((tm, tn), F32)],
            resident=[((d, tn), BF16)]),
    )(merged, w_out, x)


def _peer_score_kernel(h_ref, wq_ref, keys_ref, sc0_ref, sc1_ref):
    q = jnp.dot(h_ref[...], wq_ref[...], preferred_element_type=F32).astype(BF16)
    for hd in range(PEER_HEADS):
        for c, sc_ref in enumerate((sc0_ref, sc1_ref)):
            col = hd * PEER_QUERY_DIM + c * PEER_HALF
            sc_ref[hd] = lax.dot_general(keys_ref[hd, c].astype(BF16), q[:, col:col + PEER_HALF],
                                         (((1,), (1,)), ((), ())), preferred_element_type=F32)


def _peer_scores(h, wq, sub_keys):
    s, d = h.shape
    tm = _tile(s, 512)
    qw = PEER_HEADS * PEER_QUERY_DIM
    single = dict(pipeline_mode=pl.Buffered(1))
    out_spec = pl.BlockSpec((PEER_HEADS, PEER_N_KEYS, tm), lambda i: (0, 0, i))
    out_shape = jax.ShapeDtypeStruct((PEER_HEADS, PEER_N_KEYS, s), F32)
    return pl.pallas_call(
        _peer_score_kernel, name="peer_scores",
        grid=(s // tm,),
        in_specs=[pl.BlockSpec((tm, d), lambda i: (i, 0)),
                  pl.BlockSpec((d, qw), lambda i: (0, 0), **single),
                  pl.BlockSpec((PEER_HEADS, 2, PEER_N_KEYS, PEER_HALF), lambda i: (0, 0, 0, 0), **single)],
        out_specs=[out_spec, out_spec], out_shape=[out_shape, out_shape],
        compiler_params=_params(
            ("parallel",),
            pipelined=[((tm, d), BF16)] + [((PEER_HEADS, PEER_N_KEYS, tm), F32)] * 2,
            resident=[((d, qw), BF16), ((PEER_HEADS, 2, PEER_N_KEYS, PEER_HALF), F32),
                      ((tm, qw), F32), ((tm, qw), BF16)]),
    )(h, wq, sub_keys)


def _top_rows(x, k):
    n = x.shape[0]
    idx = lax.broadcasted_iota(jnp.int32, x.shape, 0)
    rows = []
    for _ in range(k):
        m = jnp.max(x, axis=0, keepdims=True)
        first = jnp.min(jnp.where(x == m, idx, n), axis=0, keepdims=True)
        x = jnp.where(idx == first, -jnp.inf, x)
        rows.append(m)
    return rows


def _top_rows_pair(xa, xb, k):
    n = xa.shape[0]
    idx = lax.broadcasted_iota(jnp.int32, xa.shape, 0)
    rows_a, rows_b = [], []
    for _ in range(k):
        ma = jnp.max(xa, axis=0, keepdims=True)
        mb = jnp.max(xb, axis=0, keepdims=True)
        fa = jnp.min(jnp.where(xa == ma, idx, n), axis=0, keepdims=True)
        fb = jnp.min(jnp.where(xb == mb, idx, n), axis=0, keepdims=True)
        xa = jnp.where(idx == fa, -jnp.inf, xa)
        xb = jnp.where(idx == fb, -jnp.inf, xb)
        rows_a.append(ma)
        rows_b.append(mb)
    return rows_a, rows_b


def _peer_route_kernel(sc0_ref, sc1_ref, e0_ref, e1_ref, tau_ref, cand_ref):
    n_chunks = sc0_ref.shape[-1] // LANE
    pad_rows = PEER_CAND_ROWS - len(PEER_PAIRS)
    if pad_rows:
        cand_ref[:, len(PEER_PAIRS):, :] = jnp.full((2, pad_rows, LANE), -jnp.inf, F32)

    def chunk(c, cand):
        lanes = pl.ds(pl.multiple_of(c * LANE, LANE), LANE)
        s0 = sc0_ref[0, :, lanes]
        s1 = sc1_ref[0, :, lanes]
        t0, t1 = _top_rows_pair(s0, s1, PEER_TOPK)
        for r, (a, b) in enumerate(PEER_PAIRS):
            cand[r:r + 1, :] = t0[a] + t1[b]
        best = _top_rows(cand[...], PEER_TOPK)
        z = jnp.ones_like(best[0])
        for bk in best[1:]:
            z = z + jnp.exp(bk - best[0])
        e0_ref[0, :, lanes] = jnp.exp(s0 - t0[0]) * (1.0 / z)
        e1_ref[0, :, lanes] = jnp.exp(s1 - t1[0])
        tau_ref[0, :, lanes] = best[PEER_TOPK - 1]

    def body(c, carry):
        chunk(2 * c, cand_ref.at[0])
        chunk(2 * c + 1, cand_ref.at[1])
        return carry
    lax.fori_loop(0, n_chunks // 2, body, 0)


def _peer_route(sc0, sc1):
    s = sc0.shape[-1]
    tl = _tile(s, 512)
    fac = pl.BlockSpec((1, PEER_N_KEYS, tl), lambda hd, i: (hd, 0, i))
    fac_shape = jax.ShapeDtypeStruct((PEER_HEADS, PEER_N_KEYS, s), F32)
    return pl.pallas_call(
        _peer_route_kernel, name="peer_route",
        grid=(PEER_HEADS, s // tl),
        in_specs=[fac, fac],
        out_specs=[fac, fac, pl.BlockSpec((1, 1, tl), lambda hd, i: (hd, 0, i))],
        out_shape=[fac_shape, fac_shape, jax.ShapeDtypeStruct((PEER_HEADS, 1, s), F32)],
        scratch_shapes=[pltpu.VMEM((2, PEER_CAND_ROWS, LANE), F32)],
        compiler_params=_params(
            ("parallel", "arbitrary"),
            pipelined=[((PEER_N_KEYS, tl), F32)] * 5,
            resident=[((2, PEER_CAND_ROWS, LANE), F32)]),
    )(sc0, sc1)


def _slot_row(ref, hd, i, lanes, slot, n_slots):
    per_slot = ref.shape[1] // n_slots
    row = ref[hd, i:i + 1, lanes]
    for q in range(1, n_slots):
        row = jnp.where(slot == q, ref[hd, q * per_slot + i:q * per_slot + i + 1, lanes], row)
    return row


def _peer_gate_block(s_ref, p_ref, sc0_ref, e0_ref, i, slot, n_slots, ii, lane0, sc1_ref, e1_ref, tau_ref):
    rows = slice(ii * PEER_N_KEYS, (ii + 1) * PEER_N_KEYS)
    lanes = pl.ds(lane0, LANE)
    gate = jnp.zeros((PEER_N_KEYS, LANE), F32)
    for hd in range(PEER_HEADS):
        pair = _slot_row(sc0_ref, hd, i, lanes, slot, n_slots) + sc1_ref[hd, :, lanes]
        w = _slot_row(e0_ref, hd, i, lanes, slot, n_slots) * e1_ref[hd, :, lanes]
        gate = gate + jnp.where(pair >= tau_ref[hd, :, lanes], w, 0.0)
    z = s_ref[rows, lanes]
    act = 0.5 * z * (1.0 + lax.erf(z * math.sqrt(0.5)))
    p_ref[lanes, rows] = (act * gate).T.astype(p_ref.dtype)


def _peer_expert_kernel(h_ref, u_ref, v_ref, sc0p_ref, e0p_ref, sc0c_ref, e0c_ref, sc1_ref, e1_ref,
                        tau_ref, x_ref, y_ref, s0_ref, s1_ref, p0_ref, p1_ref, ht_ref, *, n_pairs):
    te, tm = s0_ref.shape
    ni = te // PEER_N_KEYS
    n_tc = tm // LANE
    n_slots = sc0c_ref.shape[1] // (2 * ni)
    k = pl.program_id(1)

    @pl.when(k == 0)
    def _():
        y_ref[...] = x_ref[...]
        s1_ref[...] = jnp.zeros_like(s1_ref)
        p0_ref[...] = jnp.zeros_like(p0_ref)
        ht_ref[...] = h_ref[...].T

    def half(u_row0, s_new, s_old, p_new, p_old, sc0_ref, e0_ref, pair, row0):
        slot = lax.rem(pair, n_slots)
        y_ref[...] += jnp.dot(p_old[...], v_ref[u_row0:u_row0 + te, :], preferred_element_type=F32)
        for ii in range(ni):
            for tc in range(n_tc):
                _peer_gate_block(s_old, p_new, sc0_ref, e0_ref, row0 + ii, slot, n_slots, ii, tc * LANE,
                                 sc1_ref, e1_ref, tau_ref)
        s_new[...] = jnp.dot(u_ref[u_row0:u_row0 + te, :], ht_ref[...], preferred_element_type=F32)

    half(0, s0_ref, s1_ref, p1_ref, p0_ref, sc0p_ref, e0p_ref, jnp.maximum(k - 1, 0), ni)
    half(te, s1_ref, s0_ref, p0_ref, p1_ref, sc0c_ref, e0c_ref, jnp.minimum(k, n_pairs - 1), 0)


def _peer_experts(h, u, v, sc0, sc1, e0, e1, tau, x):
    s, d = h.shape
    n_exp = u.shape[0]
    tm, te = _tile(s, 512), 256
    n_pairs = n_exp // (2 * te)
    rows_per_pair = 2 * te // PEER_N_KEYS
    single = dict(pipeline_mode=pl.Buffered(1))
    cur = lambda k: jnp.minimum(k, n_pairs - 1)
    prev = lambda k: jnp.maximum(k - 1, 0)
    block_rows = max(rows_per_pair, 8)
    n_slots = block_rows // rows_per_pair
    pair_rows = lambda idx: pl.BlockSpec((PEER_HEADS, block_rows, tm),
                                         lambda t, k: (0, idx(k) // n_slots, t))
    fac = pl.BlockSpec((PEER_HEADS, PEER_N_KEYS, tm), lambda t, k: (0, 0, t), **single)
    return pl.pallas_call(
        functools.partial(_peer_expert_kernel, n_pairs=n_pairs), name="peer_experts",
        grid=(s // tm, n_pairs + 1),
        in_specs=[pl.BlockSpec((tm, d), lambda t, k: (t, 0), **single),
                  pl.BlockSpec((2 * te, d), lambda t, k: (cur(k), 0)),
                  pl.BlockSpec((2 * te, d), lambda t, k: (prev(k), 0)),
                  pair_rows(prev), pair_rows(prev), pair_rows(cur), pair_rows(cur),
                  fac, fac,
                  pl.BlockSpec((PEER_HEADS, 1, tm), lambda t, k: (0, 0, t), **single),
                  pl.BlockSpec((tm, d), lambda t, k: (t, 0), **single)],
        out_specs=pl.BlockSpec((tm, d), lambda t, k: (t, 0)),
        out_shape=jax.ShapeDtypeStruct((s, d), F32),
        scratch_shapes=[pltpu.VMEM((te, tm), F32), pltpu.VMEM((te, tm), F32),
                        pltpu.VMEM((tm, te), BF16), pltpu.VMEM((tm, te), BF16),
                        pltpu.VMEM((d, tm), BF16)],
        compiler_params=_params(
            ("parallel", "arbitrary"),
            pipelined=[((2 * te, d), BF16)] * 2 + [((tm, d), F32)] + [((PEER_HEADS, 8, tm), F32)] * 4,
            resident=[((tm, d), BF16), ((tm, d), BF16), ((tm, d), F32),
                      ((PEER_HEADS, 2, PEER_N_KEYS, tm), F32), ((2 * te, tm), F32), ((tm, 2 * te), BF16)],
            ),
    )(h, u, v, sc0, e0, sc0, e0, sc1, e1, tau, x)


def _head_gain_row(q_norm_a, k_norm_a, q_norm_b, k_norm_b):
    ones = jnp.ones((HEAD_DIM,), F32)
    segs = ([q_norm_a] * A_Q_HEADS + [k_norm_a] * A_KV_HEADS + [ones] * A_KV_HEADS
            + [q_norm_b] * B_HEADS + [k_norm_b] * B_HEADS + [ones] * B_HEADS)
    return jnp.concatenate([g.astype(F32) for g in segs]).reshape(1, IN_W)


def _layer(x, positions, norm_mix, w_in, q_norm_a, k_norm_a, sinks_a, q_norm_b, k_norm_b,
           w_branch_a, w_branch_b, w_gate, b_gate, w_out, norm_ffn, w_peer_q, peer_sub_keys,
           peer_u, peer_v):
    cos, slo, shi = _rope_tables(positions.reshape(-1, 1))
    h = _rmsnorm(x, norm_mix)
    proj = _in_projection(h, w_in, _head_gain_row(q_norm_a, k_norm_a, q_norm_b, k_norm_b), cos, slo, shi)
    gates, peer_u16, peer_v16 = _gate_projection(h, w_gate, b_gate.astype(F32), peer_u, peer_v)
    ya = _attention_a(proj, sinks_a.astype(F32))
    outs, lses = zip(*[(_attention_b_group if dil == 1 else _attention_b_dilated)(proj, g, dil)
                       for g, (_, dil) in enumerate(B_PATTERNS)])
    merged = _merge_branches(ya, outs, lses, w_branch_a.astype(BF16), w_branch_b.astype(BF16), gates)
    x = _out_projection(merged, w_out, x)

    h = _rmsnorm(x, norm_ffn)
    sc0, sc1 = _peer_scores(h, w_peer_q.astype(BF16), peer_sub_keys)
    e0, e1, tau = _peer_route(sc0, sc1)
    return _peer_experts(h, peer_u16, peer_v16, sc0, sc1, e0, e1, tau, x)


def kernel(x, positions, norm_mix, w_in, q_norm_a, k_norm_a, sinks_a, q_norm_b, k_norm_b, w_branch_a,
           w_branch_b, w_gate, b_gate, w_out, norm_ffn, w_peer_q, peer_sub_keys, peer_u, peer_v):
    batch, seq, d_model = x.shape
    assert batch == 1, "kernels are written for a single sequence"
    out = _layer(x.reshape(seq, d_model), positions.reshape(seq), norm_mix, w_in, q_norm_a, k_norm_a,
                 sinks_a, q_norm_b, k_norm_b, w_branch_a, w_branch_b, w_gate, b_gate, w_out, norm_ffn,
                 w_peer_q, peer_sub_keys, peer_u, peer_v)
    return out.reshape(batch, seq, d_model)
```

```python
import functools
import math

import jax
import jax.numpy as jnp
from jax import lax
from jax.experimental import pallas as pl
from jax.experimental.pallas import tpu as pltpu

F32 = jnp.float32
BF16 = jnp.bfloat16

LANE = 128
V7X_SCOPED_VMEM_BYTES = 60000 * 1024
COMPILER_TEMP_BYTES = 12 * 1024 * 1024

HEAD_DIM = 128
A_Q_HEADS = 16
A_KV_HEADS = 4
A_GROUP = A_Q_HEADS // A_KV_HEADS
A_WINDOW = 128
B_PATTERNS = ((128, 1), (512, 4), (2048, 16))
B_HEADS_PER_GROUP = 4
B_HEADS = B_HEADS_PER_GROUP * len(B_PATTERNS)
ROPE_THETA = 500000.0
ROPE_DIMS = HEAD_DIM // 4
ROPE_HALF = ROPE_DIMS // 2
BLOCK = 128
EPS = 1e-6
PEER_HEADS = 8
PEER_N_KEYS = 128
PEER_QUERY_DIM = 256
PEER_HALF = PEER_QUERY_DIM // 2
PEER_TOPK = 16

A_Q_W = A_Q_HEADS * HEAD_DIM
A_KV_W = A_KV_HEADS * HEAD_DIM
B_W = B_HEADS * HEAD_DIM
B_OUT_W = B_HEADS_PER_GROUP * HEAD_DIM
IN_W = A_Q_W + 2 * A_KV_W + 3 * B_W
IN_BLOCKS = IN_W // HEAD_DIM
QA_BLK, KA_BLK, VA_BLK = 0, A_Q_HEADS, A_Q_HEADS + A_KV_HEADS
QB_BLK = A_Q_HEADS + 2 * A_KV_HEADS
KB_BLK = QB_BLK + B_HEADS
VB_BLK = KB_BLK + B_HEADS

PROJ_TN = 512
PEER_PAIRS = tuple((a, b) for a in range(PEER_TOPK) for b in range(PEER_TOPK)
                   if (a + 1) * (b + 1) <= PEER_TOPK)
PEER_CAND_ROWS = -(-len(PEER_PAIRS) // 8) * 8
ROUTE_CHUNKS_PER_TRIP = 4


def _tile(n, pref):
    return pref if n % pref == 0 else n


def _nbytes(shape, dtype):
    return math.prod(shape) * jnp.dtype(dtype).itemsize


def _params(semantics, pipelined=(), resident=(), flags=None):
    need = 2 * sum(_nbytes(s, d) for s, d in pipelined) + sum(_nbytes(s, d) for s, d in resident)
    limit = min(need + COMPILER_TEMP_BYTES, V7X_SCOPED_VMEM_BYTES)
    return pltpu.CompilerParams(dimension_semantics=semantics, vmem_limit_bytes=limit, flags=flags)


def _rope_table_kernel(pos_ref, cos_ref, sin_lo_ref, sin_hi_ref):
    pos = pos_ref[...].astype(F32)
    lane = lax.broadcasted_iota(jnp.int32, (1, LANE), 1)
    k = (lane & (ROPE_HALF - 1)).astype(F32)
    inv_freq = jnp.exp(-math.log(ROPE_THETA) * k / ROPE_HALF)
    ang = pos * inv_freq
    cos, sin = jnp.cos(ang), jnp.sin(ang)
    cos_ref[...] = jnp.where(lane < ROPE_DIMS, cos, 1.0)
    sin_lo_ref[...] = jnp.where(lane < ROPE_HALF, -sin, 0.0)
    sin_hi_ref[...] = jnp.where(lane < ROPE_DIMS, jnp.where(lane >= ROPE_HALF, sin, 0.0), 0.0)


def _rope_tables(positions):
    s = positions.shape[0]
    tm = _tile(s, 1024)
    out = jax.ShapeDtypeStruct((s, LANE), F32)
    spec = pl.BlockSpec((tm, LANE), lambda i: (i, 0))
    return pl.pallas_call(
        _rope_table_kernel, name="rope_tables",
        grid=(s // tm,),
        in_specs=[pl.BlockSpec((tm, 1), lambda i: (i, 0))],
        out_specs=[spec, spec, spec], out_shape=[out, out, out],
        compiler_params=_params(("parallel",), pipelined=[((tm, LANE), F32)] * 4),
    )(positions)


def _apply_rope(x, cos, sin_lo, sin_hi):
    return (x * cos + pltpu.roll(x, LANE - ROPE_HALF, 1) * sin_lo
            + pltpu.roll(x, ROPE_HALF, 1) * sin_hi)


def _rmsnorm_kernel(x_ref, g_ref, o_ref):
    x = x_ref[...].astype(F32)
    inv = lax.rsqrt(jnp.mean(x * x, axis=-1, keepdims=True) + EPS)
    o_ref[...] = (x * inv * g_ref[...]).astype(o_ref.dtype)


def _rmsnorm(x, gain):
    s, d = x.shape
    tm = _tile(s, 256)
    return pl.pallas_call(
        _rmsnorm_kernel, name="rmsnorm",
        grid=(s // tm,),
        in_specs=[pl.BlockSpec((tm, d), lambda i: (i, 0)), pl.BlockSpec((1, d), lambda i: (0, 0))],
        out_specs=pl.BlockSpec((tm, d), lambda i: (i, 0)),
        out_shape=jax.ShapeDtypeStruct((s, d), BF16),
        compiler_params=_params(("parallel",), pipelined=[((tm, d), F32), ((tm, d), BF16)]),
    )(x, gain.reshape(1, d))


def _cast_weight_once(w_ref, wb_ref):
    @pl.when(pl.program_id(1) == 0)
    def _():
        wb_ref[...] = w_ref[...].astype(wb_ref.dtype)


def _proj_kernel(h_ref, w_ref, gain_ref, cos_ref, slo_ref, shi_ref, o_ref, wb_ref, *, row_chunk):
    _cast_weight_once(w_ref, wb_ref)
    j = pl.program_id(0)
    tiles_per = lambda blk: blk * HEAD_DIM // PROJ_TN
    is_v = ((j >= tiles_per(VA_BLK)) & (j < tiles_per(QB_BLK))) | (j >= tiles_per(VB_BLK))
    keep = jnp.where(is_v, 0.0, 1.0)
    for r in range(h_ref.shape[0] // row_chunk):
        rows = slice(r * row_chunk, (r + 1) * row_chunk)
        acc = jnp.dot(h_ref[rows, :], wb_ref[...], preferred_element_type=F32)
        cos = jnp.where(is_v, 1.0, cos_ref[rows, :])
        slo, shi = slo_ref[rows, :] * keep, shi_ref[rows, :] * keep
        for hh in range(PROJ_TN // HEAD_DIM):
            cols = slice(hh * HEAD_DIM, (hh + 1) * HEAD_DIM)
            x = acc[:, cols]
            inv = lax.rsqrt(jnp.mean(x * x, axis=-1, keepdims=True) + EPS)
            xn = x * jnp.where(is_v, 1.0, inv) * gain_ref[:, cols]
            o_ref[rows, cols] = _apply_rope(xn, cos, slo, shi).astype(o_ref.dtype)


def _in_projection(h, w_in, head_gain, cos, slo, shi):
    s, d = h.shape
    tm = _tile(s, 1024)
    row_chunk = _tile(tm, 256)
    tn = PROJ_TN
    tab = pl.BlockSpec((tm, LANE), lambda j, i: (i, 0))
    return pl.pallas_call(
        functools.partial(_proj_kernel, row_chunk=row_chunk), name="in_projection",
        grid=(IN_W // tn, s // tm),
        in_specs=[pl.BlockSpec((tm, d), lambda j, i: (i, 0)),
                  pl.BlockSpec((d, tn), lambda j, i: (0, j)),
                  pl.BlockSpec((1, tn), lambda j, i: (0, j)),
                  tab, tab, tab],
        out_specs=pl.BlockSpec((tm, tn), lambda j, i: (i, j)),
        out_shape=jax.ShapeDtypeStruct((s, IN_W), BF16),
        scratch_shapes=[pltpu.VMEM((d, tn), BF16)],
        compiler_params=_params(
            ("parallel", "arbitrary"),
            pipelined=[((tm, d), BF16), ((d, tn), F32), ((tm, tn), BF16)] + [((tm, LANE), F32)] * 3,
            resident=[((d, tn), BF16), ((row_chunk, tn), F32)]),
    )(h, w_in, head_gain, cos, slo, shi)


def _gate_kernel(h_ref, w_ref, b_ref, u_ref, v_ref, o_ref, ub_ref, vb_ref, wb_ref):
    _cast_weight_once(w_ref, wb_ref)
    acc = jnp.dot(h_ref[...], wb_ref[...], preferred_element_type=F32) + b_ref[...]
    o_ref[...] = jax.nn.sigmoid(acc).astype(o_ref.dtype)
    ub_ref[...] = u_ref[...].astype(ub_ref.dtype)
    vb_ref[...] = v_ref[...].astype(vb_ref.dtype)


def _gate_projection(h, w_gate, b_gate, peer_u, peer_v):
    s, d = h.shape
    n = w_gate.shape[1]
    tm, tn = _tile(s, 1024), _tile(n, 512)
    n_i = s // tm
    n_exp = peer_u.shape[0]
    slab = n_exp // (n_i * (n // tn))
    assert slab * n_i * (n // tn) == n_exp and slab % 16 == 0
    slab_spec = pl.BlockSpec((slab, d), lambda j, i: (j * n_i + i, 0))
    table = jax.ShapeDtypeStruct((n_exp, d), BF16)
    return pl.pallas_call(
        _gate_kernel, name="gate_projection",
        grid=(n // tn, n_i),
        in_specs=[pl.BlockSpec((tm, d), lambda j, i: (i, 0)),
                  pl.BlockSpec((d, tn), lambda j, i: (0, j)),
                  pl.BlockSpec((1, tn), lambda j, i: (0, j)),
                  slab_spec, slab_spec],
        out_specs=[pl.BlockSpec((tm, tn), lambda j, i: (i, j)), slab_spec, slab_spec],
        out_shape=[jax.ShapeDtypeStruct((s, n), BF16), table, table],
        scratch_shapes=[pltpu.VMEM((d, tn), BF16)],
        compiler_params=_params(
            ("parallel", "arbitrary"),
            pipelined=[((tm, d), BF16), ((d, tn), F32), ((tm, tn), BF16)]
            + [((slab, d), F32)] * 2 + [((slab, d), BF16)] * 2,
            resident=[((d, tn), BF16), ((tm, tn), F32)]),
    )(h, w_gate, b_gate.reshape(1, n), peer_u, peer_v)


def _band_masks(max_dist, has_prev):
    qi = lax.broadcasted_iota(jnp.int32, (BLOCK, 2 * BLOCK), 0)
    kj = lax.broadcasted_iota(jnp.int32, (BLOCK, 2 * BLOCK), 1)
    dist = BLOCK + qi - kj
    band = (dist >= 0) & (dist <= max_dist)
    return band, band & ((kj >= BLOCK) | has_prev)


def _band_block(q, kw, vw, mask, sink=None):
    s = lax.dot_general(q, kw, (((1,), (1,)), ((), ())), preferred_element_type=F32) * HEAD_DIM ** -0.5
    s = jnp.where(mask, s, -jnp.inf)
    m = jnp.max(s, axis=-1, keepdims=True)
    if sink is not None:
        m = jnp.maximum(m, sink)
    p = jnp.exp(s - m)
    denom = jnp.sum(p, axis=-1, keepdims=True)
    if sink is not None:
        denom = denom + jnp.exp(sink - m)
    o = jnp.dot(p.astype(vw.dtype), vw, preferred_element_type=F32) * (1.0 / denom)
    return o, m + jnp.log(denom)


def _dilated_attn_kernel(q_ref, k_ref, kp_ref, v_ref, vp_ref, o_ref, lse_ref,
                         qf_ref, kf_ref, vf_ref, of_ref, lf_ref, *, dilation, max_dist):
    tile_rows = q_ref.shape[0]
    qf_ref[...] = q_ref[...].astype(F32)
    kf_ref[:tile_rows, :] = kp_ref[...].astype(F32)
    kf_ref[tile_rows:, :] = k_ref[...].astype(F32)
    vf_ref[:tile_rows, :] = vp_ref[...].astype(F32)
    vf_ref[tile_rows:, :] = v_ref[...].astype(F32)
    _, mask = _band_masks(max_dist, pl.program_id(1) > 0)
    for r in range(dilation):
        q = qf_ref[pl.ds(r, BLOCK, stride=dilation), :].astype(BF16)
        kw = kf_ref[pl.ds(r, 2 * BLOCK, stride=dilation), :].astype(BF16)
        vw = vf_ref[pl.ds(r, 2 * BLOCK, stride=dilation), :].astype(BF16)
        o, lse = _band_block(q, kw, vw, mask)
        of_ref[pl.ds(r, BLOCK, stride=dilation), :] = o
        lf_ref[pl.ds(r, BLOCK, stride=dilation), :] = jnp.broadcast_to(lse, (BLOCK, HEAD_DIM))
    o_ref[...] = of_ref[...].astype(o_ref.dtype)
    lse_ref[...] = lf_ref[...]


def _attention_b_dilated(proj, group, dilation):
    s = proj.shape[0]
    rows = dilation * BLOCK
    h0 = group * B_HEADS_PER_GROUP
    cur = lambda blk: pl.BlockSpec((rows, HEAD_DIM), lambda hh, i: (i, blk + h0 + hh))
    prev = lambda blk: pl.BlockSpec((rows, HEAD_DIM), lambda hh, i: (jnp.maximum(i - 1, 0), blk + h0 + hh))
    out_spec = pl.BlockSpec((rows, HEAD_DIM), lambda hh, i: (i, hh))
    window = B_PATTERNS[group][0]
    kern = functools.partial(_dilated_attn_kernel, dilation=dilation, max_dist=window // dilation)
    f32_tile = lambda n: pltpu.VMEM((n * rows, HEAD_DIM), F32)
    return pl.pallas_call(
        kern, name=f"attention_b{group}",
        grid=(B_HEADS_PER_GROUP, s // rows),
        in_specs=[cur(QB_BLK), cur(KB_BLK), prev(KB_BLK), cur(VB_BLK), prev(VB_BLK)],
        out_specs=[out_spec, out_spec],
        out_shape=[jax.ShapeDtypeStruct((s, B_OUT_W), BF16), jax.ShapeDtypeStruct((s, B_OUT_W), F32)],
        scratch_shapes=[f32_tile(1), f32_tile(2), f32_tile(2), f32_tile(1), f32_tile(1)],
        compiler_params=_params(
            ("parallel", "arbitrary"),
            pipelined=[((rows, HEAD_DIM), BF16)] * 6 + [((rows, HEAD_DIM), F32)],
            resident=[((7 * rows, HEAD_DIM), F32)]),
    )(proj, proj, proj, proj, proj)


def _band_attn_kernel(*refs, n_group, max_dist, has_sink, with_lse, tq, tile_axis):
    refs = list(refs)
    sink_ref = refs.pop(0) if has_sink else None
    q_ref, k_ref, kp_ref, v_ref, vp_ref, o_ref = refs[:6]
    lse_ref = refs[6] if with_lse else None
    head0 = pl.program_id(0) * n_group

    k_ext = jnp.concatenate([kp_ref[...], k_ref[...]], axis=0)
    v_ext = jnp.concatenate([vp_ref[...], v_ref[...]], axis=0)
    band, first_mask = _band_masks(max_dist, pl.program_id(tile_axis) > 0)

    for b in range(tq // BLOCK):
        rows = slice(b * BLOCK, (b + 1) * BLOCK)
        kw = k_ext[b * BLOCK:(b + 2) * BLOCK]
        vw = v_ext[b * BLOCK:(b + 2) * BLOCK]
        mask = first_mask if b == 0 else band
        for g in range(n_group):
            cols = slice(g * HEAD_DIM, (g + 1) * HEAD_DIM)
            sink = sink_ref[head0 + g] if has_sink else None
            o, lse = _band_block(q_ref[rows, cols], kw, vw, mask, sink)
            o_ref[rows, cols] = o.astype(o_ref.dtype)
            if with_lse:
                lse_ref[rows, cols] = jnp.broadcast_to(lse, (BLOCK, HEAD_DIM))


def _prev_block(i, tq):
    return jnp.maximum(i * (tq // BLOCK) - 1, 0)


def _attention_a(proj, sinks):
    s = proj.shape[0]
    tq = _tile(s, 512)
    gw = A_GROUP * HEAD_DIM
    cur = lambda blk: pl.BlockSpec((tq, HEAD_DIM), lambda kv, i: (i, blk + kv))
    prev = lambda blk: pl.BlockSpec((BLOCK, HEAD_DIM), lambda kv, i: (_prev_block(i, tq), blk + kv))
    kern = functools.partial(_band_attn_kernel, n_group=A_GROUP, max_dist=A_WINDOW - 1,
                             has_sink=True, with_lse=False, tq=tq, tile_axis=1)
    return pl.pallas_call(
        kern, name="attention_a",
        grid=(A_KV_HEADS, s // tq),
        in_specs=[pl.BlockSpec(memory_space=pltpu.SMEM),
                  pl.BlockSpec((tq, gw), lambda kv, i: (i, kv)),
                  cur(KA_BLK), prev(KA_BLK), cur(VA_BLK), prev(VA_BLK)],
        out_specs=pl.BlockSpec((tq, gw), lambda kv, i: (i, kv)),
        out_shape=jax.ShapeDtypeStruct((s, A_Q_W), BF16),
        compiler_params=_params(
            ("parallel", "arbitrary"),
            pipelined=[((tq, gw), BF16)] * 2 + [((tq, HEAD_DIM), BF16)] * 2 + [((BLOCK, HEAD_DIM), BF16)] * 2),
    )(sinks, proj, proj, proj, proj, proj)


def _attention_b_group(proj, group, dilation):
    s = proj.shape[0]
    sub_len = s // dilation
    view = proj.reshape(sub_len, dilation * IN_W)
    tq = _tile(sub_len, 512)
    h0 = group * B_HEADS_PER_GROUP
    col = lambda blk: (lambda r, hh, i: (i, r * IN_BLOCKS + blk + h0 + hh))
    pcol = lambda blk: (lambda r, hh, i: (_prev_block(i, tq), r * IN_BLOCKS + blk + h0 + hh))
    cur = lambda blk: pl.BlockSpec((tq, HEAD_DIM), col(blk))
    prev = lambda blk: pl.BlockSpec((BLOCK, HEAD_DIM), pcol(blk))
    out_spec = pl.BlockSpec((tq, HEAD_DIM), lambda r, hh, i: (i, r * B_HEADS_PER_GROUP + hh))
    window = B_PATTERNS[group][0]
    kern = functools.partial(_band_attn_kernel, n_group=1, max_dist=window // dilation,
                             has_sink=False, with_lse=True, tq=tq, tile_axis=2)
    out, lse = pl.pallas_call(
        kern, name=f"attention_b{group}",
        grid=(dilation, B_HEADS_PER_GROUP, sub_len // tq),
        in_specs=[cur(QB_BLK), cur(KB_BLK), prev(KB_BLK), cur(VB_BLK), prev(VB_BLK)],
        out_specs=[out_spec, out_spec],
        out_shape=[jax.ShapeDtypeStruct((sub_len, dilation * B_OUT_W), BF16),
                   jax.ShapeDtypeStruct((sub_len, dilation * B_OUT_W), F32)],
        compiler_params=_params(
            ("parallel", "parallel", "arbitrary"),
            pipelined=[((tq, HEAD_DIM), BF16)] * 4 + [((BLOCK, HEAD_DIM), BF16)] * 2 + [((tq, HEAD_DIM), F32)]),
    )(view, view, view, view, view)
    return out.reshape(s, B_OUT_W), lse.reshape(s, B_OUT_W)


def _merge_kernel(ya_ref, o0_ref, o1_ref, o2_ref, l0_ref, l1_ref, l2_ref,
                  wa_ref, wb_ref, ga_ref, gb_ref, out_ref, yb_ref):
    @pl.when(pl.program_id(1) == 0)
    def _():
        ls = [l0_ref[...], l1_ref[...], l2_ref[...]]
        m = jnp.maximum(jnp.maximum(ls[0], ls[1]), ls[2])
        es = [jnp.exp(l - m) for l in ls]
        inv = 1.0 / (es[0] + es[1] + es[2])
        os = [o0_ref[...].astype(F32), o1_ref[...].astype(F32), o2_ref[...].astype(F32)]
        yb = (es[0] * inv) * os[0] + (es[1] * inv) * os[1] + (es[2] * inv) * os[2]
        yb_ref[...] = yb.astype(yb_ref.dtype)

    a = jnp.dot(ya_ref[...], wa_ref[...], preferred_element_type=F32)
    b = jnp.dot(yb_ref[...], wb_ref[...], preferred_element_type=F32)
    out = ga_ref[...].astype(F32) * a + gb_ref[...].astype(F32) * b
    out_ref[...] = out.astype(out_ref.dtype)


def _merge_branches(ya, outs, lses, wa, wb, gates):
    s = ya.shape[0]
    d = wa.shape[1]
    tm, tn = _tile(s, 512), _tile(d, 1024)
    nj = d // tn
    row = lambda w: pl.BlockSpec((tm, w), lambda i, j: (i, 0))
    return pl.pallas_call(
        _merge_kernel, name="merge_branches",
        grid=(s // tm, nj),
        in_specs=[row(A_Q_W)] + [row(B_OUT_W)] * 6 + [
            pl.BlockSpec((A_Q_W, tn), lambda i, j: (0, j)),
            pl.BlockSpec((B_OUT_W, tn), lambda i, j: (0, j)),
            pl.BlockSpec((tm, tn), lambda i, j: (i, j)),
            pl.BlockSpec((tm, tn), lambda i, j: (i, nj + j))],
        out_specs=pl.BlockSpec((tm, tn), lambda i, j: (i, j)),
        out_shape=jax.ShapeDtypeStruct((s, d), BF16),
        scratch_shapes=[pltpu.VMEM((tm, B_OUT_W), BF16)],
        compiler_params=_params(
            ("parallel", "arbitrary"),
            pipelined=[((tm, A_Q_W), BF16)] + [((tm, B_OUT_W), BF16)] * 3 + [((tm, B_OUT_W), F32)] * 3
            + [((A_Q_W, tn), BF16), ((B_OUT_W, tn), BF16)] + [((tm, tn), BF16)] * 3,
            resident=[((tm, B_OUT_W), BF16), ((tm, tn), F32), ((tm, tn), F32)]),
    )(ya, *outs, *lses, wa, wb, gates, gates)


def _out_proj_kernel(m_ref, w_ref, x_ref, o_ref, wb_ref):
    _cast_weight_once(w_ref, wb_ref)
    o_ref[...] = x_ref[...] + jnp.dot(m_ref[...], wb_ref[...], preferred_element_type=F32)


def _out_projection(merged, w_out, x):
    s, d = x.shape
    tm, tn = _tile(s, 1024), _tile(d, 512)
    return pl.pallas_call(
        _out_proj_kernel, name="out_projection",
        grid=(d // tn, s // tm),
        in_specs=[pl.BlockSpec((tm, d), lambda j, i: (i, 0)),
                  pl.BlockSpec((d, tn), lambda j, i: (0, j)),
                  pl.BlockSpec((tm, tn), lambda j, i: (i, j))],
        out_specs=pl.BlockSpec((tm, tn), lambda j, i: (i, j)),
        out_shape=jax.ShapeDtypeStruct((s, d), F32),
        scratch_shapes=[pltpu.VMEM((d, tn), BF16)],
        compiler_params=_params(
            ("parallel", "arbitrary"),
            pipelined=[((tm, d), BF16), ((d, tn), F32), ((tm, tn), F32), ((tm, tn), F32)],
            resident=[((d, tn), BF16)]),
    )(merged, w_out, x)


def _peer_score_kernel(h_ref, wq_ref, keys_ref, sc0_ref, sc1_ref):
    q = jnp.dot(h_ref[...], wq_ref[...], preferred_element_type=F32).astype(BF16)
    for hd in range(PEER_HEADS):
        for c, sc_ref in enumerate((sc0_ref, sc1_ref)):
            col = hd * PEER_QUERY_DIM + c * PEER_HALF
            sc_ref[hd] = lax.dot_general(keys_ref[hd, c].astype(BF16), q[:, col:col + PEER_HALF],
                                         (((1,), (1,)), ((), ())), preferred_element_type=F32)


def _peer_scores(h, wq, sub_keys):
    s, d = h.shape
    tm = _tile(s, 512)
    qw = PEER_HEADS * PEER_QUERY_DIM
    single = dict(pipeline_mode=pl.Buffered(1))
    out_spec = pl.BlockSpec((PEER_HEADS, PEER_N_KEYS, tm), lambda i: (0, 0, i))
    out_shape = jax.ShapeDtypeStruct((PEER_HEADS, PEER_N_KEYS, s), F32)
    return pl.pallas_call(
        _peer_score_kernel, name="peer_scores",
        grid=(s // tm,),
        in_specs=[pl.BlockSpec((tm, d), lambda i: (i, 0)),
                  pl.BlockSpec((d, qw), lambda i: (0, 0), **single),
                  pl.BlockSpec((PEER_HEADS, 2, PEER_N_KEYS, PEER_HALF), lambda i: (0, 0, 0, 0), **single)],
        out_specs=[out_spec, out_spec], out_shape=[out_shape, out_shape],
        compiler_params=_params(
            ("parallel",),
            pipelined=[((tm, d), BF16)] + [((PEER_HEADS, PEER_N_KEYS, tm), F32)] * 2,
            resident=[((d, qw), BF16), ((PEER_HEADS, 2, PEER_N_KEYS, PEER_HALF), F32),
                      ((tm, qw), F32), ((tm, qw), BF16)]),
    )(h, wq, sub_keys)


def _top_rows(x, k):
    n = x.shape[0]
    idx = lax.broadcasted_iota(jnp.int32, x.shape, 0)
    rows = []
    for _ in range(k):
        m = jnp.max(x, axis=0, keepdims=True)
        first = jnp.min(jnp.where(x == m, idx, n), axis=0, keepdims=True)
        x = jnp.where(idx == first, -jnp.inf, x)
        rows.append(m)
    return rows


def _top_rows_pair(xa, xb, k):
    n = xa.shape[0]
    idx = lax.broadcasted_iota(jnp.int32, xa.shape, 0)
    rows_a, rows_b = [], []
    for _ in range(k):
        ma = jnp.max(xa, axis=0, keepdims=True)
        mb = jnp.max(xb, axis=0, keepdims=True)
        fa = jnp.min(jnp.where(xa == ma, idx, n), axis=0, keepdims=True)
        fb = jnp.min(jnp.where(xb == mb, idx, n), axis=0, keepdims=True)
        xa = jnp.where(idx == fa, -jnp.inf, xa)
        xb = jnp.where(idx == fb, -jnp.inf, xb)
        rows_a.append(ma)
        rows_b.append(mb)
    return rows_a, rows_b


def _peer_route_kernel(sc0_ref, sc1_ref, e0_ref, e1_ref, tau_ref, cand_ref):
    n_chunks = sc0_ref.shape[-1] // LANE
    pad_rows = PEER_CAND_ROWS - len(PEER_PAIRS)
    if pad_rows:
        cand_ref[:, len(PEER_PAIRS):, :] = jnp.full((ROUTE_CHUNKS_PER_TRIP, pad_rows, LANE), -jnp.inf, F32)

    def chunk(c, cand):
        lanes = pl.ds(pl.multiple_of(c * LANE, LANE), LANE)
        s0 = sc0_ref[0, :, lanes]
        s1 = sc1_ref[0, :, lanes]
        t0, t1 = _top_rows_pair(s0, s1, PEER_TOPK)
        for r, (a, b) in enumerate(PEER_PAIRS):
            cand[r:r + 1, :] = t0[a] + t1[b]
        best = _top_rows(cand[...], PEER_TOPK)
        z = jnp.ones_like(best[0])
        for bk in best[1:]:
            z = z + jnp.exp(bk - best[0])
        e0_ref[0, :, lanes] = jnp.exp(s0 - t0[0]) * (1.0 / z)
        e1_ref[0, :, lanes] = jnp.exp(s1 - t1[0])
        tau_ref[0, :, lanes] = best[PEER_TOPK - 1]

    def body(c, carry):
        for q in range(ROUTE_CHUNKS_PER_TRIP):
            chunk(ROUTE_CHUNKS_PER_TRIP * c + q, cand_ref.at[q])
        return carry
    lax.fori_loop(0, n_chunks // ROUTE_CHUNKS_PER_TRIP, body, 0)


def _peer_route(sc0, sc1):
    s = sc0.shape[-1]
    tl = _tile(s, 1024)
    fac = pl.BlockSpec((1, PEER_N_KEYS, tl), lambda hd, i: (hd, 0, i))
    fac_shape = jax.ShapeDtypeStruct((PEER_HEADS, PEER_N_KEYS, s), F32)
    return pl.pallas_call(
        _peer_route_kernel, name="peer_route",
        grid=(PEER_HEADS, s // tl),
        in_specs=[fac, fac],
        out_specs=[fac, fac, pl.BlockSpec((1, 1, tl), lambda hd, i: (hd, 0, i))],
        out_shape=[fac_shape, fac_shape, jax.ShapeDtypeStruct((PEER_HEADS, 1, s), F32)],
        scratch_shapes=[pltpu.VMEM((ROUTE_CHUNKS_PER_TRIP, PEER_CAND_ROWS, LANE), F32)],
        compiler_params=_params(
            ("parallel", "arbitrary"),
            pipelined=[((PEER_N_KEYS, tl), F32)] * 5,
            resident=[((ROUTE_CHUNKS_PER_TRIP, PEER_CAND_ROWS, LANE), F32)]),
    )(sc0, sc1)


def _slot_row(ref, hd, i, lanes, slot, n_slots):
    per_slot = ref.shape[1] // n_slots
    row = ref[hd, i:i + 1, lanes]
    for q in range(1, n_slots):
        row = jnp.where(slot == q, ref[hd, q * per_slot + i:q * per_slot + i + 1, lanes], row)
    return row


def _peer_gate_block(s_ref, p_ref, sc0_ref, e0_ref, i, slot, n_slots, ii, lane0, sc1_ref, e1_ref, tau_ref):
    rows = slice(ii * PEER_N_KEYS, (ii + 1) * PEER_N_KEYS)
    lanes = pl.ds(lane0, LANE)
    gate = jnp.zeros((PEER_N_KEYS, LANE), F32)
    for hd in range(PEER_HEADS):
        pair = _slot_row(sc0_ref, hd, i, lanes, slot, n_slots) + sc1_ref[hd, :, lanes]
        w = _slot_row(e0_ref, hd, i, lanes, slot, n_slots) * e1_ref[hd, :, lanes]
        gate = gate + jnp.where(pair >= tau_ref[hd, :, lanes], w, 0.0)
    z = s_ref[rows, lanes]
    act = 0.5 * z * (1.0 + lax.erf(z * math.sqrt(0.5)))
    p_ref[lanes, rows] = (act * gate).T.astype(p_ref.dtype)


def _peer_expert_kernel(h_ref, u_ref, v_ref, sc0p_ref, e0p_ref, sc0c_ref, e0c_ref, sc1_ref, e1_ref,
                        tau_ref, x_ref, y_ref, s0_ref, s1_ref, p0_ref, p1_ref, ht_ref, *, n_pairs):
    te, tm = s0_ref.shape
    ni = te // PEER_N_KEYS
    n_tc = tm // LANE
    n_slots = sc0c_ref.shape[1] // (2 * ni)
    k = pl.program_id(1)

    @pl.when(k == 0)
    def _():
        y_ref[...] = x_ref[...]
        s1_ref[...] = jnp.zeros_like(s1_ref)
        p0_ref[...] = jnp.zeros_like(p0_ref)
        ht_ref[...] = h_ref[...].T

    def half(u_row0, s_new, s_old, p_new, p_old, sc0_ref, e0_ref, pair, row0):
        slot = lax.rem(pair, n_slots)
        y_ref[...] += jnp.dot(p_old[...], v_ref[u_row0:u_row0 + te, :], preferred_element_type=F32)
        for ii in range(ni):
            for tc in range(n_tc):
                _peer_gate_block(s_old, p_new, sc0_ref, e0_ref, row0 + ii, slot, n_slots, ii, tc * LANE,
                                 sc1_ref, e1_ref, tau_ref)
        s_new[...] = jnp.dot(u_ref[u_row0:u_row0 + te, :], ht_ref[...], preferred_element_type=F32)

    half(0, s0_ref, s1_ref, p1_ref, p0_ref, sc0p_ref, e0p_ref, jnp.maximum(k - 1, 0), ni)
    half(te, s1_ref, s0_ref, p0_ref, p1_ref, sc0c_ref, e0c_ref, jnp.minimum(k, n_pairs - 1), 0)


def _peer_experts(h, u, v, sc0, sc1, e0, e1, tau, x):
    s, d = h.shape
    n_exp = u.shape[0]
    tm, te = _tile(s, 512), 256
    n_pairs = n_exp // (2 * te)
    rows_per_pair = 2 * te // PEER_N_KEYS
    single = dict(pipeline_mode=pl.Buffered(1))
    cur = lambda k: jnp.minimum(k, n_pairs - 1)
    prev = lambda k: jnp.maximum(k - 1, 0)
    block_rows = max(rows_per_pair, 8)
    n_slots = block_rows // rows_per_pair
    pair_rows = lambda idx: pl.BlockSpec((PEER_HEADS, block_rows, tm),
                                         lambda t, k: (0, idx(k) // n_slots, t))
    fac = pl.BlockSpec((PEER_HEADS, PEER_N_KEYS, tm), lambda t, k: (0, 0, t), **single)
    return pl.pallas_call(
        functools.partial(_peer_expert_kernel, n_pairs=n_pairs), name="peer_experts",
        grid=(s // tm, n_pairs + 1),
        in_specs=[pl.BlockSpec((tm, d), lambda t, k: (t, 0), **single),
                  pl.BlockSpec((2 * te, d), lambda t, k: (cur(k), 0)),
                  pl.BlockSpec((2 * te, d), lambda t, k: (prev(k), 0)),
                  pair_rows(prev), pair_rows(prev), pair_rows(cur), pair_rows(cur),
                  fac, fac,
                  pl.BlockSpec((PEER_HEADS, 1, tm), lambda t, k: (0, 0, t), **single),
                  pl.BlockSpec((tm, d), lambda t, k: (t, 0), **single)],
        out_specs=pl.BlockSpec((tm, d), lambda t, k: (t, 0)),
        out_shape=jax.ShapeDtypeStruct((s, d), F32),
        scratch_shapes=[pltpu.VMEM((te, tm), F32), pltpu.VMEM((te, tm), F32),
                        pltpu.VMEM((tm, te), BF16), pltpu.VMEM((tm, te), BF16),
                        pltpu.VMEM((d, tm), BF16)],
        compiler_params=_params(
            ("parallel", "arbitrary"),
            pipelined=[((2 * te, d), BF16)] * 2 + [((tm, d), F32)] + [((PEER_HEADS, 8, tm), F32)] * 4,
            resident=[((tm, d), BF16), ((tm, d), BF16), ((tm, d), F32),
                      ((PEER_HEADS, 2, PEER_N_KEYS, tm), F32), ((2 * te, tm), F32), ((tm, 2 * te), BF16)],
            ),
    )(h, u, v, sc0, e0, sc0, e0, sc1, e1, tau, x)


def _head_gain_row(q_norm_a, k_norm_a, q_norm_b, k_norm_b):
    ones = jnp.ones((HEAD_DIM,), F32)
    segs = ([q_norm_a] * A_Q_HEADS + [k_norm_a] * A_KV_HEADS + [ones] * A_KV_HEADS
            + [q_norm_b] * B_HEADS + [k_norm_b] * B_HEADS + [ones] * B_HEADS)
    return jnp.concatenate([g.astype(F32) for g in segs]).reshape(1, IN_W)


def _layer(x, positions, norm_mix, w_in, q_norm_a, k_norm_a, sinks_a, q_norm_b, k_norm_b,
           w_branch_a, w_branch_b, w_gate, b_gate, w_out, norm_ffn, w_peer_q, peer_sub_keys,
           peer_u, peer_v):
    cos, slo, shi = _rope_tables(positions.reshape(-1, 1))
    h = _rmsnorm(x, norm_mix)
    proj = _in_projection(h, w_in, _head_gain_row(q_norm_a, k_norm_a, q_norm_b, k_norm_b), cos, slo, shi)
    gates, peer_u16, peer_v16 = _gate_projection(h, w_gate, b_gate.astype(F32), peer_u, peer_v)
    ya = _attention_a(proj, sinks_a.astype(F32))
    outs, lses = zip(*[(_attention_b_group if dil == 1 else _attention_b_dilated)(proj, g, dil)
                       for g, (_, dil) in enumerate(B_PATTERNS)])
    merged = _merge_branches(ya, outs, lses, w_branch_a.astype(BF16), w_branch_b.astype(BF16), gates)
    x = _out_projection(merged, w_out, x)

    h = _rmsnorm(x, norm_ffn)
    sc0, sc1 = _peer_scores(h, w_peer_q.astype(BF16), peer_sub_keys)
    e0, e1, tau = _peer_route(sc0, sc1)
    return _peer_experts(h, peer_u16, peer_v16, sc0, sc1, e0, e1, tau, x)


def kernel(x, positions, norm_mix, w_in, q_norm_a, k_norm_a, sinks_a, q_norm_b, k_norm_b, w_branch_a,
           w_branch_b, w_gate, b_gate, w_out, norm_ffn, w_peer_q, peer_sub_keys, peer_u, peer_v):
    batch, seq, d_model = x.shape
    assert batch == 1, "kernels are written for a single sequence"
    out = _layer(x.reshape(seq, d_model), positions.reshape(seq), norm_mix, w_in, q_norm_a, k_norm_a,
                 sinks_a, q_norm_b, k_norm_b, w_branch_a, w_branch_b, w_gate, b_gate, w_out, norm_ffn,
                 w_peer_q, peer_sub_keys, peer_u, peer_v)
    return out.reshape(batch, seq, d_model)
```

```python
import functools
import math

import jax
import jax.numpy as jnp
from jax import lax
from jax.experimental import pallas as pl
from jax.experimental.pallas import tpu as pltpu

F32 = jnp.float32
BF16 = jnp.bfloat16

LANE = 128
V7X_SCOPED_VMEM_BYTES = 60000 * 1024
COMPILER_TEMP_BYTES = 12 * 1024 * 1024

HEAD_DIM = 128
A_Q_HEADS = 16
A_KV_HEADS = 4
A_GROUP = A_Q_HEADS // A_KV_HEADS
A_WINDOW = 128
B_PATTERNS = ((128, 1), (512, 4), (2048, 16))
B_HEADS_PER_GROUP = 4
B_HEADS = B_HEADS_PER_GROUP * len(B_PATTERNS)
ROPE_THETA = 500000.0
ROPE_DIMS = HEAD_DIM // 4
ROPE_HALF = ROPE_DIMS // 2
BLOCK = 128
EPS = 1e-6
PEER_HEADS = 8
PEER_N_KEYS = 128
PEER_QUERY_DIM = 256
PEER_HALF = PEER_QUERY_DIM // 2
PEER_TOPK = 16

A_Q_W = A_Q_HEADS * HEAD_DIM
A_KV_W = A_KV_HEADS * HEAD_DIM
B_W = B_HEADS * HEAD_DIM
B_OUT_W = B_HEADS_PER_GROUP * HEAD_DIM
IN_W = A_Q_W + 2 * A_KV_W + 3 * B_W
IN_BLOCKS = IN_W // HEAD_DIM
QA_BLK, KA_BLK, VA_BLK = 0, A_Q_HEADS, A_Q_HEADS + A_KV_HEADS
QB_BLK = A_Q_HEADS + 2 * A_KV_HEADS
KB_BLK = QB_BLK + B_HEADS
VB_BLK = KB_BLK + B_HEADS

PROJ_TN = 512
PEER_PAIRS = tuple((a, b) for a in range(PEER_TOPK) for b in range(PEER_TOPK)
                   if (a + 1) * (b + 1) <= PEER_TOPK)
PEER_CAND_ROWS = -(-len(PEER_PAIRS) // 8) * 8
ROUTE_CHUNKS_PER_TRIP = 4


def _tile(n, pref):
    return pref if n % pref == 0 else n


def _nbytes(shape, dtype):
    return math.prod(shape) * jnp.dtype(dtype).itemsize


def _params(semantics, pipelined=(), resident=(), flags=None):
    need = 2 * sum(_nbytes(s, d) for s, d in pipelined) + sum(_nbytes(s, d) for s, d in resident)
    limit = min(need + COMPILER_TEMP_BYTES, V7X_SCOPED_VMEM_BYTES)
    return pltpu.CompilerParams(dimension_semantics=semantics, vmem_limit_bytes=limit, flags=flags)


def _rope_table_kernel(pos_ref, cos_ref, sin_lo_ref, sin_hi_ref):
    pos = pos_ref[...].astype(F32)
    lane = lax.broadcasted_iota(jnp.int32, (1, LANE), 1)
    k = (lane & (ROPE_HALF - 1)).astype(F32)
    inv_freq = jnp.exp(-math.log(ROPE_THETA) * k / ROPE_HALF)
    ang = pos * inv_freq
    cos, sin = jnp.cos(ang), jnp.sin(ang)
    cos_ref[...] = jnp.where(lane < ROPE_DIMS, cos, 1.0)
    sin_lo_ref[...] = jnp.where(lane < ROPE_HALF, -sin, 0.0)
    sin_hi_ref[...] = jnp.where(lane < ROPE_DIMS, jnp.where(lane >= ROPE_HALF, sin, 0.0), 0.0)


def _rope_tables(positions):
    s = positions.shape[0]
    tm = _tile(s, 1024)
    out = jax.ShapeDtypeStruct((s, LANE), F32)
    spec = pl.BlockSpec((tm, LANE), lambda i: (i, 0))
    return pl.pallas_call(
        _rope_table_kernel, name="rope_tables",
        grid=(s // tm,),
        in_specs=[pl.BlockSpec((tm, 1), lambda i: (i, 0))],
        out_specs=[spec, spec, spec], out_shape=[out, out, out],
        compiler_params=_params(("parallel",), pipelined=[((tm, LANE), F32)] * 4),
    )(positions)


def _apply_rope(x, cos, sin_lo, sin_hi):
    return (x * cos + pltpu.roll(x, LANE - ROPE_HALF, 1) * sin_lo
            + pltpu.roll(x, ROPE_HALF, 1) * sin_hi)


def _rmsnorm_kernel(x_ref, g_ref, o_ref):
    x = x_ref[...].astype(F32)
    inv = lax.rsqrt(jnp.mean(x * x, axis=-1, keepdims=True) + EPS)
    o_ref[...] = (x * inv * g_ref[...]).astype(o_ref.dtype)


def _rmsnorm(x, gain):
    s, d = x.shape
    tm = _tile(s, 256)
    return pl.pallas_call(
        _rmsnorm_kernel, name="rmsnorm",
        grid=(s // tm,),
        in_specs=[pl.BlockSpec((tm, d), lambda i: (i, 0)), pl.BlockSpec((1, d), lambda i: (0, 0))],
        out_specs=pl.BlockSpec((tm, d), lambda i: (i, 0)),
        out_shape=jax.ShapeDtypeStruct((s, d), BF16),
        compiler_params=_params(("parallel",), pipelined=[((tm, d), F32), ((tm, d), BF16)]),
    )(x, gain.reshape(1, d))


def _cast_weight_once(w_ref, wb_ref):
    @pl.when(pl.program_id(1) == 0)
    def _():
        wb_ref[...] = w_ref[...].astype(wb_ref.dtype)


def _proj_kernel(h_ref, w_ref, gain_ref, cos_ref, slo_ref, shi_ref, o_ref, wb_ref, *, row_chunk):
    _cast_weight_once(w_ref, wb_ref)
    j = pl.program_id(0)
    tiles_per = lambda blk: blk * HEAD_DIM // PROJ_TN
    is_v = ((j >= tiles_per(VA_BLK)) & (j < tiles_per(QB_BLK))) | (j >= tiles_per(VB_BLK))
    keep = jnp.where(is_v, 0.0, 1.0)
    for r in range(h_ref.shape[0] // row_chunk):
        rows = slice(r * row_chunk, (r + 1) * row_chunk)
        acc = jnp.dot(h_ref[rows, :], wb_ref[...], preferred_element_type=F32)
        cos = jnp.where(is_v, 1.0, cos_ref[rows, :])
        slo, shi = slo_ref[rows, :] * keep, shi_ref[rows, :] * keep
        for hh in range(PROJ_TN // HEAD_DIM):
            cols = slice(hh * HEAD_DIM, (hh + 1) * HEAD_DIM)
            x = acc[:, cols]
            inv = lax.rsqrt(jnp.mean(x * x, axis=-1, keepdims=True) + EPS)
            xn = x * jnp.where(is_v, 1.0, inv) * gain_ref[:, cols]
            o_ref[rows, cols] = _apply_rope(xn, cos, slo, shi).astype(o_ref.dtype)


def _in_projection(h, w_in, head_gain, cos, slo, shi):
    s, d = h.shape
    tm = _tile(s, 1024)
    row_chunk = _tile(tm, 256)
    tn = PROJ_TN
    tab = pl.BlockSpec((tm, LANE), lambda j, i: (i, 0))
    return pl.pallas_call(
        functools.partial(_proj_kernel, row_chunk=row_chunk), name="in_projection",
        grid=(IN_W // tn, s // tm),
        in_specs=[pl.BlockSpec((tm, d), lambda j, i: (i, 0)),
                  pl.BlockSpec((d, tn), lambda j, i: (0, j)),
                  pl.BlockSpec((1, tn), lambda j, i: (0, j)),
                  tab, tab, tab],
        out_specs=pl.BlockSpec((tm, tn), lambda j, i: (i, j)),
        out_shape=jax.ShapeDtypeStruct((s, IN_W), BF16),
        scratch_shapes=[pltpu.VMEM((d, tn), BF16)],
        compiler_params=_params(
            ("parallel", "arbitrary"),
            pipelined=[((tm, d), BF16), ((d, tn), F32), ((tm, tn), BF16)] + [((tm, LANE), F32)] * 3,
            resident=[((d, tn), BF16), ((row_chunk, tn), F32)]),
    )(h, w_in, head_gain, cos, slo, shi)


def _gate_kernel(h_ref, w_ref, b_ref, u_ref, v_ref, o_ref, ub_ref, vb_ref, wb_ref):
    _cast_weight_once(w_ref, wb_ref)
    acc = jnp.dot(h_ref[...], wb_ref[...], preferred_element_type=F32) + b_ref[...]
    o_ref[...] = jax.nn.sigmoid(acc).astype(o_ref.dtype)
    ub_ref[...] = u_ref[...].astype(ub_ref.dtype)
    vb_ref[...] = v_ref[...].astype(vb_ref.dtype)


def _gate_projection(h, w_gate, b_gate, peer_u, peer_v):
    s, d = h.shape
    n = w_gate.shape[1]
    tm, tn = _tile(s, 1024), _tile(n, 512)
    n_i = s // tm
    n_exp = peer_u.shape[0]
    slab = n_exp // (n_i * (n // tn))
    assert slab * n_i * (n // tn) == n_exp and slab % 16 == 0
    slab_spec = pl.BlockSpec((slab, d), lambda j, i: (j * n_i + i, 0))
    table = jax.ShapeDtypeStruct((n_exp, d), BF16)
    return pl.pallas_call(
        _gate_kernel, name="gate_projection",
        grid=(n // tn, n_i),
        in_specs=[pl.BlockSpec((tm, d), lambda j, i: (i, 0)),
                  pl.BlockSpec((d, tn), lambda j, i: (0, j)),
                  pl.BlockSpec((1, tn), lambda j, i: (0, j)),
                  slab_spec, slab_spec],
        out_specs=[pl.BlockSpec((tm, tn), lambda j, i: (i, j)), slab_spec, slab_spec],
        out_shape=[jax.ShapeDtypeStruct((s, n), BF16), table, table],
        scratch_shapes=[pltpu.VMEM((d, tn), BF16)],
        compiler_params=_params(
            ("parallel", "arbitrary"),
            pipelined=[((tm, d), BF16), ((d, tn), F32), ((tm, tn), BF16)]
            + [((slab, d), F32)] * 2 + [((slab, d), BF16)] * 2,
            resident=[((d, tn), BF16), ((tm, tn), F32)]),
    )(h, w_gate, b_gate.reshape(1, n), peer_u, peer_v)


def _band_masks(max_dist, has_prev):
    qi = lax.broadcasted_iota(jnp.int32, (BLOCK, 2 * BLOCK), 0)
    kj = lax.broadcasted_iota(jnp.int32, (BLOCK, 2 * BLOCK), 1)
    dist = BLOCK + qi - kj
    band = (dist >= 0) & (dist <= max_dist)
    return band, band & ((kj >= BLOCK) | has_prev)


def _band_block(q, kw, vw, mask, sink=None):
    s = lax.dot_general(q, kw, (((1,), (1,)), ((), ())), preferred_element_type=F32) * HEAD_DIM ** -0.5
    s = jnp.where(mask, s, -jnp.inf)
    m = jnp.max(s, axis=-1, keepdims=True)
    if sink is not None:
        m = jnp.maximum(m, sink)
    p = jnp.exp(s - m)
    denom = jnp.sum(p, axis=-1, keepdims=True)
    if sink is not None:
        denom = denom + jnp.exp(sink - m)
    o = jnp.dot(p.astype(vw.dtype), vw, preferred_element_type=F32) * (1.0 / denom)
    return o, m + jnp.log(denom)


def _dilated_attn_kernel(q_ref, k_ref, kp_ref, v_ref, vp_ref, o_ref, lse_ref,
                         qf_ref, kf_ref, vf_ref, of_ref, lf_ref, *, dilation, max_dist):
    tile_rows = q_ref.shape[0]
    qf_ref[...] = q_ref[...].astype(F32)
    kf_ref[:tile_rows, :] = kp_ref[...].astype(F32)
    kf_ref[tile_rows:, :] = k_ref[...].astype(F32)
    vf_ref[:tile_rows, :] = vp_ref[...].astype(F32)
    vf_ref[tile_rows:, :] = v_ref[...].astype(F32)
    _, mask = _band_masks(max_dist, pl.program_id(1) > 0)
    for r in range(dilation):
        q = qf_ref[pl.ds(r, BLOCK, stride=dilation), :].astype(BF16)
        kw = kf_ref[pl.ds(r, 2 * BLOCK, stride=dilation), :].astype(BF16)
        vw = vf_ref[pl.ds(r, 2 * BLOCK, stride=dilation), :].astype(BF16)
        o, lse = _band_block(q, kw, vw, mask)
        of_ref[pl.ds(r, BLOCK, stride=dilation), :] = o
        lf_ref[pl.ds(r, BLOCK, stride=dilation), :] = jnp.broadcast_to(lse, (BLOCK, HEAD_DIM))
    o_ref[...] = of_ref[...].astype(o_ref.dtype)
    lse_ref[...] = lf_ref[...]


def _attention_b_dilated(proj, group, dilation):
    s = proj.shape[0]
    rows = dilation * BLOCK
    h0 = group * B_HEADS_PER_GROUP
    cur = lambda blk: pl.BlockSpec((rows, HEAD_DIM), lambda hh, i: (i, blk + h0 + hh))
    prev = lambda blk: pl.BlockSpec((rows, HEAD_DIM), lambda hh, i: (jnp.maximum(i - 1, 0), blk + h0 + hh))
    out_spec = pl.BlockSpec((rows, HEAD_DIM), lambda hh, i: (i, hh))
    window = B_PATTERNS[group][0]
    kern = functools.partial(_dilated_attn_kernel, dilation=dilation, max_dist=window // dilation)
    f32_tile = lambda n: pltpu.VMEM((n * rows, HEAD_DIM), F32)
    return pl.pallas_call(
        kern, name=f"attention_b{group}",
        grid=(B_HEADS_PER_GROUP, s // rows),
        in_specs=[cur(QB_BLK), cur(KB_BLK), prev(KB_BLK), cur(VB_BLK), prev(VB_BLK)],
        out_specs=[out_spec, out_spec],
        out_shape=[jax.ShapeDtypeStruct((s, B_OUT_W), BF16), jax.ShapeDtypeStruct((s, B_OUT_W), F32)],
        scratch_shapes=[f32_tile(1), f32_tile(2), f32_tile(2), f32_tile(1), f32_tile(1)],
        compiler_params=_params(
            ("parallel", "arbitrary"),
            pipelined=[((rows, HEAD_DIM), BF16)] * 6 + [((rows, HEAD_DIM), F32)],
            resident=[((7 * rows, HEAD_DIM), F32)]),
    )(proj, proj, proj, proj, proj)


def _band_attn_kernel(*refs, n_group, max_dist, has_sink, with_lse, tq, tile_axis):
    refs = list(refs)
    sink_ref = refs.pop(0) if has_sink else None
    q_ref, k_ref, kp_ref, v_ref, vp_ref, o_ref = refs[:6]
    lse_ref = refs[6] if with_lse else None
    head0 = pl.program_id(0) * n_group

    k_ext = jnp.concatenate([kp_ref[...], k_ref[...]], axis=0)
    v_ext = jnp.concatenate([vp_ref[...], v_ref[...]], axis=0)
    band, first_mask = _band_masks(max_dist, pl.program_id(tile_axis) > 0)

    for b in range(tq // BLOCK):
        rows = slice(b * BLOCK, (b + 1) * BLOCK)
        kw = k_ext[b * BLOCK:(b + 2) * BLOCK]
        vw = v_ext[b * BLOCK:(b + 2) * BLOCK]
        mask = first_mask if b == 0 else band
        for g in range(n_group):
            cols = slice(g * HEAD_DIM, (g + 1) * HEAD_DIM)
            sink = sink_ref[head0 + g] if has_sink else None
            o, lse = _band_block(q_ref[rows, cols], kw, vw, mask, sink)
            o_ref[rows, cols] = o.astype(o_ref.dtype)
            if with_lse:
                lse_ref[rows, cols] = jnp.broadcast_to(lse, (BLOCK, HEAD_DIM))


def _prev_block(i, tq):
    return jnp.maximum(i * (tq // BLOCK) - 1, 0)


def _attention_a(proj, sinks):
    s = proj.shape[0]
    tq = _tile(s, 512)
    gw = A_GROUP * HEAD_DIM
    cur = lambda blk: pl.BlockSpec((tq, HEAD_DIM), lambda kv, i: (i, blk + kv))
    prev = lambda blk: pl.BlockSpec((BLOCK, HEAD_DIM), lambda kv, i: (_prev_block(i, tq), blk + kv))
    kern = functools.partial(_band_attn_kernel, n_group=A_GROUP, max_dist=A_WINDOW - 1,
                             has_sink=True, with_lse=False, tq=tq, tile_axis=1)
    return pl.pallas_call(
        kern, name="attention_a",
        grid=(A_KV_HEADS, s // tq),
        in_specs=[pl.BlockSpec(memory_space=pltpu.SMEM),
                  pl.BlockSpec((tq, gw), lambda kv, i: (i, kv)),
                  cur(KA_BLK), prev(KA_BLK), cur(VA_BLK), prev(VA_BLK)],
        out_specs=pl.BlockSpec((tq, gw), lambda kv, i: (i, kv)),
        out_shape=jax.ShapeDtypeStruct((s, A_Q_W), BF16),
        compiler_params=_params(
            ("parallel", "arbitrary"),
            pipelined=[((tq, gw), BF16)] * 2 + [((tq, HEAD_DIM), BF16)] * 2 + [((BLOCK, HEAD_DIM), BF16)] * 2),
    )(sinks, proj, proj, proj, proj, proj)


def _attention_b_group(proj, group, dilation):
    s = proj.shape[0]
    sub_len = s // dilation
    view = proj.reshape(sub_len, dilation * IN_W)
    tq = _tile(sub_len, 512)
    h0 = group * B_HEADS_PER_GROUP
    col = lambda blk: (lambda r, hh, i: (i, r * IN_BLOCKS + blk + h0 + hh))
    pcol = lambda blk: (lambda r, hh, i: (_prev_block(i, tq), r * IN_BLOCKS + blk + h0 + hh))
    cur = lambda blk: pl.BlockSpec((tq, HEAD_DIM), col(blk))
    prev = lambda blk: pl.BlockSpec((BLOCK, HEAD_DIM), pcol(blk))
    out_spec = pl.BlockSpec((tq, HEAD_DIM), lambda r, hh, i: (i, r * B_HEADS_PER_GROUP + hh))
    window = B_PATTERNS[group][0]
    kern = functools.partial(_band_attn_kernel, n_group=1, max_dist=window // dilation,
                             has_sink=False, with_lse=True, tq=tq, tile_axis=2)
    out, lse = pl.pallas_call(
        kern, name=f"attention_b{group}",
        grid=(dilation, B_HEADS_PER_GROUP, sub_len // tq),
        in_specs=[cur(QB_BLK), cur(KB_BLK), prev(KB_BLK), cur(VB_BLK), prev(VB_BLK)],
        out_specs=[out_spec, out_spec],
        out_shape=[jax.ShapeDtypeStruct((sub_len, dilation * B_OUT_W), BF16),
                   jax.ShapeDtypeStruct((sub_len, dilation * B_OUT_W), F32)],
        compiler_params=_params(
            ("parallel", "parallel", "arbitrary"),
            pipelined=[((tq, HEAD_DIM), BF16)] * 4 + [((BLOCK, HEAD_DIM), BF16)] * 2 + [((tq, HEAD_DIM), F32)]),
    )(view, view, view, view, view)
    return out.reshape(s, B_OUT_W), lse.reshape(s, B_OUT_W)


def _merge_kernel(ya_ref, o0_ref, o1_ref, o2_ref, l0_ref, l1_ref, l2_ref,
                  wa_ref, wb_ref, ga_ref, gb_ref, out_ref, yb_ref):
    @pl.when(pl.program_id(1) == 0)
    def _():
        ls = [l0_ref[...], l1_ref[...], l2_ref[...]]
        m = jnp.maximum(jnp.maximum(ls[0], ls[1]), ls[2])
        es = [jnp.exp(l - m) for l in ls]
        inv = 1.0 / (es[0] + es[1] + es[2])
        os = [o0_ref[...].astype(F32), o1_ref[...].astype(F32), o2_ref[...].astype(F32)]
        yb = (es[0] * inv) * os[0] + (es[1] * inv) * os[1] + (es[2] * inv) * os[2]
        yb_ref[...] = yb.astype(yb_ref.dtype)

    a = jnp.dot(ya_ref[...], wa_ref[...], preferred_element_type=F32)
    b = jnp.dot(yb_ref[...], wb_ref[...], preferred_element_type=F32)
    out = ga_ref[...].astype(F32) * a + gb_ref[...].astype(F32) * b
    out_ref[...] = out.astype(out_ref.dtype)


def _merge_branches(ya, outs, lses, wa, wb, gates):
    s = ya.shape[0]
    d = wa.shape[1]
    tm, tn = _tile(s, 1024), _tile(d, 512)
    nj = d // tn
    row = lambda w: pl.BlockSpec((tm, w), lambda i, j: (i, 0))
    return pl.pallas_call(
        _merge_kernel, name="merge_branches",
        grid=(s // tm, nj),
        in_specs=[row(A_Q_W)] + [row(B_OUT_W)] * 6 + [
            pl.BlockSpec((A_Q_W, tn), lambda i, j: (0, j)),
            pl.BlockSpec((B_OUT_W, tn), lambda i, j: (0, j)),
            pl.BlockSpec((tm, tn), lambda i, j: (i, j)),
            pl.BlockSpec((tm, tn), lambda i, j: (i, nj + j))],
        out_specs=pl.BlockSpec((tm, tn), lambda i, j: (i, j)),
        out_shape=jax.ShapeDtypeStruct((s, d), BF16),
        scratch_shapes=[pltpu.VMEM((tm, B_OUT_W), BF16)],
        compiler_params=_params(
            ("parallel", "arbitrary"),
            pipelined=[((tm, A_Q_W), BF16)] + [((tm, B_OUT_W), BF16)] * 3 + [((tm, B_OUT_W), F32)] * 3
            + [((A_Q_W, tn), BF16), ((B_OUT_W, tn), BF16)] + [((tm, tn), BF16)] * 3,
            resident=[((tm, B_OUT_W), BF16), ((tm, tn), F32), ((tm, tn), F32)]),
    )(ya, *outs, *lses, wa, wb, gates, gates)


def _out_proj_kernel(m_ref, w_ref, x_ref, o_ref, wb_ref):
    _cast_weight_once(w_ref, wb_ref)
    o_ref[...] = x_ref[...] + jnp.dot(m_ref[...], wb_ref[...], preferred_element_type=F32)


def _out_projection(merged, w_out, x):
    s, d = x.shape
    tm, tn = _tile(s, 1024), _tile(d, 512)
    return pl.pallas_call(
        _out_proj_kernel, name="out_projection",
        grid=(d // tn, s // tm),
        in_specs=[pl.BlockSpec((tm, d), lambda j, i: (i, 0)),
                  pl.BlockSpec((d, tn), lambda j, i: (0, j)),
                  pl.BlockSpec((tm, tn), lambda j, i: (i, j))],
        out_specs=pl.BlockSpec((tm, tn), lambda j, i: (i, j)),
        out_shape=jax.ShapeDtypeStruct((s, d), F32),
        scratch_shapes=[pltpu.VMEM((d, tn), BF16)],
        compiler_params=_params(
            ("parallel", "arbitrary"),
            pipelined=[((tm, d), BF16), ((d, tn), F32), ((tm, tn), F32), ((tm, tn), F32)],
            resident=[((d, tn), BF16)]),
    )(merged, w_out, x)


def _peer_score_kernel(h_ref, wq_ref, keys_ref, sc0_ref, sc1_ref):
    q = jnp.dot(h_ref[...], wq_ref[...], preferred_element_type=F32).astype(BF16)
    for hd in range(PEER_HEADS):
        for c, sc_ref in enumerate((sc0_ref, sc1_ref)):
            col = hd * PEER_QUERY_DIM + c * PEER_HALF
            sc_ref[hd] = lax.dot_general(keys_ref[hd, c].astype(BF16), q[:, col:col + PEER_HALF],
                                         (((1,), (1,)), ((), ())), preferred_element_type=F32)


def _peer_scores(h, wq, sub_keys):
    s, d = h.shape
    tm = _tile(s, 512)
    qw = PEER_HEADS * PEER_QUERY_DIM
    single = dict(pipeline_mode=pl.Buffered(1))
    out_spec = pl.BlockSpec((PEER_HEADS, PEER_N_KEYS, tm), lambda i: (0, 0, i))
    out_shape = jax.ShapeDtypeStruct((PEER_HEADS, PEER_N_KEYS, s), F32)
    return pl.pallas_call(
        _peer_score_kernel, name="peer_scores",
        grid=(s // tm,),
        in_specs=[pl.BlockSpec((tm, d), lambda i: (i, 0)),
                  pl.BlockSpec((d, qw), lambda i: (0, 0), **single),
                  pl.BlockSpec((PEER_HEADS, 2, PEER_N_KEYS, PEER_HALF), lambda i: (0, 0, 0, 0), **single)],
        out_specs=[out_spec, out_spec], out_shape=[out_shape, out_shape],
        compiler_params=_params(
            ("parallel",),
            pipelined=[((tm, d), BF16)] + [((PEER_HEADS, PEER_N_KEYS, tm), F32)] * 2,
            resident=[((d, qw), BF16), ((PEER_HEADS, 2, PEER_N_KEYS, PEER_HALF), F32),
                      ((tm, qw), F32), ((tm, qw), BF16)]),
    )(h, wq, sub_keys)


def _top_rows(x, k):
    n = x.shape[0]
    idx = lax.broadcasted_iota(jnp.int32, x.shape, 0)
    rows = []
    for _ in range(k):
        m = jnp.max(x, axis=0, keepdims=True)
        first = jnp.min(jnp.where(x == m, idx, n), axis=0, keepdims=True)
        x = jnp.where(idx == first, -jnp.inf, x)
        rows.append(m)
    return rows


def _top_rows_pair(xa, xb, k):
    n = xa.shape[0]
    idx = lax.broadcasted_iota(jnp.int32, xa.shape, 0)
    rows_a, rows_b = [], []
    for _ in range(k):
        ma = jnp.max(xa, axis=0, keepdims=True)
        mb = jnp.max(xb, axis=0, keepdims=True)
        fa = jnp.min(jnp.where(xa == ma, idx, n), axis=0, keepdims=True)
        fb = jnp.min(jnp.where(xb == mb, idx, n), axis=0, keepdims=True)
        xa = jnp.where(idx == fa, -jnp.inf, xa)
        xb = jnp.where(idx == fb, -jnp.inf, xb)
        rows_a.append(ma)
        rows_b.append(mb)
    return rows_a, rows_b


def _peer_route_kernel(sc0_ref, sc1_ref, e0_ref, e1_ref, tau_ref, cand_ref):
    n_chunks = sc0_ref.shape[-1] // LANE
    pad_rows = PEER_CAND_ROWS - len(PEER_PAIRS)
    if pad_rows:
        cand_ref[:, len(PEER_PAIRS):, :] = jnp.full((ROUTE_CHUNKS_PER_TRIP, pad_rows, LANE), -jnp.inf, F32)

    def chunk(c, cand):
        lanes = pl.ds(pl.multiple_of(c * LANE, LANE), LANE)
        s0 = sc0_ref[0, :, lanes]
        s1 = sc1_ref[0, :, lanes]
        t0, t1 = _top_rows_pair(s0, s1, PEER_TOPK)
        for r, (a, b) in enumerate(PEER_PAIRS):
            cand[r:r + 1, :] = t0[a] + t1[b]
        best = _top_rows(cand[...], PEER_TOPK)
        z = jnp.ones_like(best[0])
        for bk in best[1:]:
            z = z + jnp.exp(bk - best[0])
        e0_ref[0, :, lanes] = jnp.exp(s0 - t0[0]) * (1.0 / z)
        e1_ref[0, :, lanes] = jnp.exp(s1 - t1[0])
        tau_ref[0, :, lanes] = best[PEER_TOPK - 1]

    def body(c, carry):
        for q in range(ROUTE_CHUNKS_PER_TRIP):
            chunk(ROUTE_CHUNKS_PER_TRIP * c + q, cand_ref.at[q])
        return carry
    lax.fori_loop(0, n_chunks // ROUTE_CHUNKS_PER_TRIP, body, 0)


def _peer_route(sc0, sc1):
    s = sc0.shape[-1]
    tl = _tile(s, 1024)
    fac = pl.BlockSpec((1, PEER_N_KEYS, tl), lambda hd, i: (hd, 0, i))
    fac_shape = jax.ShapeDtypeStruct((PEER_HEADS, PEER_N_KEYS, s), F32)
    return pl.pallas_call(
        _peer_route_kernel, name="peer_route",
        grid=(PEER_HEADS, s // tl),
        in_specs=[fac, fac],
        out_specs=[fac, fac, pl.BlockSpec((1, 1, tl), lambda hd, i: (hd, 0, i))],
        out_shape=[fac_shape, fac_shape, jax.ShapeDtypeStruct((PEER_HEADS, 1, s), F32)],
        scratch_shapes=[pltpu.VMEM((ROUTE_CHUNKS_PER_TRIP, PEER_CAND_ROWS, LANE), F32)],
        compiler_params=_params(
            ("parallel", "arbitrary"),
            pipelined=[((PEER_N_KEYS, tl), F32)] * 5,
            resident=[((ROUTE_CHUNKS_PER_TRIP, PEER_CAND_ROWS, LANE), F32)]),
    )(sc0, sc1)


def _slot_row(ref, hd, i, lanes, slot, n_slots):
    per_slot = ref.shape[1] // n_slots
    row = ref[hd, i:i + 1, lanes]
    for q in range(1, n_slots):
        row = jnp.where(slot == q, ref[hd, q * per_slot + i:q * per_slot + i + 1, lanes], row)
    return row


def _peer_gate_block(s_ref, p_ref, sc0_ref, e0_ref, i, slot, n_slots, ii, lane0, sc1_ref, e1_ref, tau_ref):
    rows = slice(ii * PEER_N_KEYS, (ii + 1) * PEER_N_KEYS)
    lanes = pl.ds(lane0, LANE)
    gate = jnp.zeros((PEER_N_KEYS, LANE), F32)
    for hd in range(PEER_HEADS):
        pair = _slot_row(sc0_ref, hd, i, lanes, slot, n_slots) + sc1_ref[hd, :, lanes]
        w = _slot_row(e0_ref, hd, i, lanes, slot, n_slots) * e1_ref[hd, :, lanes]
        gate = gate + jnp.where(pair >= tau_ref[hd, :, lanes], w, 0.0)
    z = s_ref[rows, lanes]
    act = 0.5 * z * (1.0 + lax.erf(z * math.sqrt(0.5)))
    p_ref[lanes, rows] = (act * gate).T.astype(p_ref.dtype)


def _peer_expert_kernel(h_ref, u_ref, v_ref, sc0p_ref, e0p_ref, sc0c_ref, e0c_ref, sc1_ref, e1_ref,
                        tau_ref, x_ref, y_ref, s0_ref, s1_ref, p0_ref, p1_ref, ht_ref, *, n_pairs):
    te, tm = s0_ref.shape
    ni = te // PEER_N_KEYS
    n_tc = tm // LANE
    n_slots = sc0c_ref.shape[1] // (2 * ni)
    k = pl.program_id(1)

    @pl.when(k == 0)
    def _():
        y_ref[...] = x_ref[...]
        s1_ref[...] = jnp.zeros_like(s1_ref)
        p0_ref[...] = jnp.zeros_like(p0_ref)
        ht_ref[...] = h_ref[...].T

    def half(u_row0, s_new, s_old, p_new, p_old, sc0_ref, e0_ref, pair, row0):
        slot = lax.rem(pair, n_slots)
        y_ref[...] += jnp.dot(p_old[...], v_ref[u_row0:u_row0 + te, :], preferred_element_type=F32)
        for ii in range(ni):
            for tc in range(n_tc):
                _peer_gate_block(s_old, p_new, sc0_ref, e0_ref, row0 + ii, slot, n_slots, ii, tc * LANE,
                                 sc1_ref, e1_ref, tau_ref)
        s_new[...] = jnp.dot(u_ref[u_row0:u_row0 + te, :], ht_ref[...], preferred_element_type=F32)

    half(0, s0_ref, s1_ref, p1_ref, p0_ref, sc0p_ref, e0p_ref, jnp.maximum(k - 1, 0), ni)
    half(te, s1_ref, s0_ref, p0_ref, p1_ref, sc0c_ref, e0c_ref, jnp.minimum(k, n_pairs - 1), 0)


def _peer_experts(h, u, v, sc0, sc1, e0, e1, tau, x):
    s, d = h.shape
    n_exp = u.shape[0]
    tm, te = _tile(s, 512), 256
    n_pairs = n_exp // (2 * te)
    rows_per_pair = 2 * te // PEER_N_KEYS
    single = dict(pipeline_mode=pl.Buffered(1))
    cur = lambda k: jnp.minimum(k, n_pairs - 1)
    prev = lambda k: jnp.maximum(k - 1, 0)
    block_rows = max(rows_per_pair, 8)
    n_slots = block_rows // rows_per_pair
    pair_rows = lambda idx: pl.BlockSpec((PEER_HEADS, block_rows, tm),
                                         lambda t, k: (0, idx(k) // n_slots, t))
    fac = pl.BlockSpec((PEER_HEADS, PEER_N_KEYS, tm), lambda t, k: (0, 0, t), **single)
    return pl.pallas_call(
        functools.partial(_peer_expert_kernel, n_pairs=n_pairs), name="peer_experts",
        grid=(s // tm, n_pairs + 1),
        in_specs=[pl.BlockSpec((tm, d), lambda t, k: (t, 0), **single),
                  pl.BlockSpec((2 * te, d), lambda t, k: (cur(k), 0)),
                  pl.BlockSpec((2 * te, d), lambda t, k: (prev(k), 0)),
                  pair_rows(prev), pair_rows(prev), pair_rows(cur), pair_rows(cur),
                  fac, fac,
                  pl.BlockSpec((PEER_HEADS, 1, tm), lambda t, k: (0, 0, t), **single),
                  pl.BlockSpec((tm, d), lambda t, k: (t, 0), **single)],
        out_specs=pl.BlockSpec((tm, d), lambda t, k: (t, 0)),
        out_shape=jax.ShapeDtypeStruct((s, d), F32),
        scratch_shapes=[pltpu.VMEM((te, tm), F32), pltpu.VMEM((te, tm), F32),
                        pltpu.VMEM((tm, te), BF16), pltpu.VMEM((tm, te), BF16),
                        pltpu.VMEM((d, tm), BF16)],
        compiler_params=_params(
            ("parallel", "arbitrary"),
            pipelined=[((2 * te, d), BF16)] * 2 + [((tm, d), F32)] + [((PEER_HEADS, 8, tm), F32)] * 4,
            resident=[((tm, d), BF16), ((tm, d), BF16), ((tm, d), F32),
                      ((PEER_HEADS, 2, PEER_N_KEYS, tm), F32), ((2 * te, tm), F32), ((tm, 2 * te), BF16)],
            ),
    )(h, u, v, sc0, e0, sc0, e0, sc1, e1, tau, x)


def _head_gain_row(q_norm_a, k_norm_a, q_norm_b, k_norm_b):
    ones = jnp.ones((HEAD_DIM,), F32)
    segs = ([q_norm_a] * A_Q_HEADS + [k_norm_a] * A_KV_HEADS + [ones] * A_KV_HEADS
            + [q_norm_b] * B_HEADS + [k_norm_b] * B_HEADS + [ones] * B_HEADS)
    return jnp.concatenate([g.astype(F32) for g in segs]).reshape(1, IN_W)


def _layer(x, positions, norm_mix, w_in, q_norm_a, k_norm_a, sinks_a, q_norm_b, k_norm_b,
           w_branch_a, w_branch_b, w_gate, b_gate, w_out, norm_ffn, w_peer_q, peer_sub_keys,
           peer_u, peer_v):
    cos, slo, shi = _rope_tables(positions.reshape(-1, 1))
    h = _rmsnorm(x, norm_mix)
    proj = _in_projection(h, w_in, _head_gain_row(q_norm_a, k_norm_a, q_norm_b, k_norm_b), cos, slo, shi)
    gates, peer_u16, peer_v16 = _gate_projection(h, w_gate, b_gate.astype(F32), peer_u, peer_v)
    ya = _attention_a(proj, sinks_a.astype(F32))
    outs, lses = zip(*[(_attention_b_group if dil == 1 else _attention_b_dilated)(proj, g, dil)
                       for g, (_, dil) in enumerate(B_PATTERNS)])
    merged = _merge_branches(ya, outs, lses, w_branch_a.astype(BF16), w_branch_b.astype(BF16), gates)
    x = _out_projection(merged, w_out, x)

    h = _rmsnorm(x, norm_ffn)
    sc0, sc1 = _peer_scores(h, w_peer_q.astype(BF16), peer_sub_keys)
    e0, e1, tau = _peer_route(sc0, sc1)
    return _peer_experts(h, peer_u16, peer_v16, sc0, sc1, e0, e1, tau, x)


def kernel(x, positions, norm_mix, w_in, q_norm_a, k_norm_a, sinks_a, q_norm_b, k_norm_b, w_branch_a,
           w_branch_b, w_gate, b_gate, w_out, norm_ffn, w_peer_q, peer_sub_keys, peer_u, peer_v):
    batch, seq, d_model = x.shape
    assert batch == 1, "kernels are written for a single sequence"
    out = _layer(x.reshape(seq, d_model), positions.reshape(seq), norm_mix, w_in, q_norm_a, k_norm_a,
                 sinks_a, q_norm_b, k_norm_b, w_branch_a, w_branch_b, w_gate, b_gate, w_out, norm_ffn,
                 w_peer_q, peer_sub_keys, peer_u, peer_v)
    return out.reshape(batch, seq, d_model)
```
